```python
import math
import jax, jax.numpy as jnp
from jax import lax
import numpy as np

D_MODEL = 1024
BATCH = 8
SEQ = 2048
DEPTH = 4
DEC_BATCH = 32
DEC_SEQ = 2048
PAST_LEN = 128

A_HEADS = 8
A_HEAD_DIM = 64
B_GROUPS = ((128, 1), (512, 4), (2048, 16))
N_GROUPS = 3
B_HEADS = 8
B_HEAD_DIM = 64
X_HEADS = 4
X_HEAD_DIM = 64
N_MEM = 256
D_FF = 2816
CONV_WIDTH = 3
ROPE_THETA = 500000.0
Q_BLOCK = 128
BAND_BLOCK = 64
EPS = 1e-6
NEG_INF = -1e30
N_A_LAYERS = (DEPTH + 1) // 2
N_B_LAYERS = DEPTH // 2
X_WIDTH = X_HEADS * X_HEAD_DIM
A_IN = 4 * A_HEADS * A_HEAD_DIM + 2 * A_HEADS * A_HEAD_DIM + X_WIDTH
A_OUT = 2 * A_HEADS * A_HEAD_DIM + X_WIDTH
B_IN = 3 * N_GROUPS * B_HEADS * B_HEAD_DIM + X_WIDTH
B_OUT = B_HEADS * B_HEAD_DIM + X_WIDTH

kernel_name = 'hybrid_diff_dilated_memory_encoder'


def rms_norm(x, g):
    xf = x.astype(jnp.float32)
    y = xf * lax.rsqrt(jnp.mean(xf * xf, axis=-1, keepdims=True) + EPS)
    return (y * g.astype(jnp.float32)).astype(x.dtype)


def rope_tables(S, dh):
    rot = dh // 4
    half = rot // 2
    inv = ROPE_THETA ** (-(jnp.arange(half, dtype=jnp.float32) * 2.0 / rot))
    ang = jnp.arange(S, dtype=jnp.float32)[:, None] * inv[None, :]
    return jnp.cos(ang), jnp.sin(ang)


def apply_rope(t, cos, sin):
    half = cos.shape[-1]
    rot = 2 * half
    tf = t[..., :rot].astype(jnp.float32)
    t1, t2 = tf[..., :half], tf[..., half:]
    c = cos[None, :, None, :]
    s = sin[None, :, None, :]
    r = jnp.concatenate([t1 * c - t2 * s, t2 * c + t1 * s], axis=-1).astype(t.dtype)
    return jnp.concatenate([r, t[..., rot:]], axis=-1)


def diff_mixer(p, cos, sin, q_g, k_g, lam_p, subln_g, lam_init):
    B, S, _ = p.shape
    H, dh = A_HEADS, A_HEAD_DIM
    qk = p[..., :4 * H * dh].reshape(B, S, 4 * H, dh)
    q = apply_rope(rms_norm(qk[:, :, :2 * H], q_g), cos, sin)
    k = apply_rope(rms_norm(qk[:, :, 2 * H:], k_g), cos, sin)
    q = q.reshape(B, S, 2, H, dh).transpose(2, 0, 1, 3, 4)
    k = k.reshape(B, S, 2, H, dh).transpose(2, 0, 1, 3, 4)
    v = p[..., 4 * H * dh:].reshape(B, S, H, 2 * dh)
    lp = lam_p.astype(jnp.float32)
    lam = jnp.exp(jnp.sum(lp[0] * lp[1])) - jnp.exp(jnp.sum(lp[2] * lp[3])) + lam_init
    scale = 1.0 / math.sqrt(dh)
    nb = S // Q_BLOCK
    qb = q.reshape(2, B, nb, Q_BLOCK, H, dh).transpose(2, 0, 1, 3, 4, 5)

    def block(qblk):
        s = jnp.einsum('mbqhd,mbkhd->mbhqk', qblk, k).astype(jnp.float32) * scale
        pr = jax.nn.softmax(s, axis=-1)
        a = pr[0] - lam * pr[1]
        return jnp.einsum('bhqk,bkhe->bqhe', a.astype(v.dtype), v)

    o = lax.map(block, qb)
    o = o.transpose(1, 0, 2, 3, 4).reshape(B, S, H, 2 * dh)
    o = rms_norm(o, subln_g) * (1.0 - lam_init)
    return o.reshape(B, S, H * 2 * dh)


def dilated_group(q, k, v, dil, radius):
    B, S, H, dh = q.shape
    L = S // dil

    def to_res(t):
        return t.reshape(B, L, dil, H, dh).transpose(0, 2, 1, 3, 4).reshape(B * dil, L, H, dh)

    qr, kr, vr = to_res(q), to_res(k), to_res(v)
    bq = math.gcd(L, BAND_BLOCK)
    nb = L // bq
    W = bq + 2 * radius
    pad = ((0, 0), (radius, radius), (0, 0), (0, 0))
    kp = jnp.pad(kr, pad)
    vp = jnp.pad(vr, pad)
    idx = jnp.arange(nb)[:, None] * bq + jnp.arange(W)[None, :]
    kb = kp[:, idx]
    vb = vp[:, idx]
    qb = qr.reshape(B * dil, nb, bq, H, dh)
    s = jnp.einsum('bnqhd,bnkhd->bnhqk', qb, kb).astype(jnp.float32) * (1.0 / math.sqrt(dh))
    kpos = idx - radius
    qpos = jnp.arange(nb)[:, None] * bq + jnp.arange(bq)[None, :]
    rel = kpos[:, None, :] - qpos[:, :, None]
    valid = (jnp.abs(rel) <= radius) & (kpos[:, None, :] >= 0) & (kpos[:, None, :] < L)
    s = jnp.where(valid[None, :, None], s, NEG_INF)
    lse = jax.nn.logsumexp(s, axis=-1)
    pr = jnp.exp(s - lse[..., None])
    o = jnp.einsum('bnhqk,bnkhd->bnqhd', pr.astype(v.dtype), vb)
    o = o.reshape(B, dil, L, H, dh).transpose(0, 2, 1, 3, 4).reshape(B, S, H, dh)
    lse = lse.transpose(0, 1, 3, 2).reshape(B, dil, L, H).transpose(0, 2, 1, 3).reshape(B, S, H)
    return o, lse


def dilated_mixer(p, cos, sin, q_g, k_g):
    B, S, _ = p.shape
    G, H, dh = N_GROUPS, B_HEADS, B_HEAD_DIM
    n = G * H * dh
    q = p[..., :n].reshape(B, S, G, H, dh)
    k = p[..., n:2 * n].reshape(B, S, G, H, dh)
    v = p[..., 2 * n:].reshape(B, S, G, H, dh)
    outs, lses = [], []
    for g, (window, dil) in enumerate(B_GROUPS):
        qg = apply_rope(rms_norm(q[:, :, g], q_g[g]), cos, sin)
        kg = apply_rope(rms_norm(k[:, :, g], k_g[g]), cos, sin)
        o, lse = dilated_group(qg, kg, v[:, :, g], dil, (window // 2) // dil)
        outs.append(o)
        lses.append(lse)
    w = jax.nn.softmax(jnp.stack(lses, axis=0), axis=0)
    o = jnp.einsum('gbsh,gbshd->bshd', w.astype(p.dtype), jnp.stack(outs, axis=0))
    return o.reshape(B, S, H * dh)


def memory_attention(xq, mk, mv):
    s = jnp.einsum('bqhd,bkhd->bhqk', xq, mk).astype(jnp.float32) * (1.0 / math.sqrt(X_HEAD_DIM))
    pr = jax.nn.softmax(s, axis=-1)
    return jnp.einsum('bhqk,bkhd->bqhd', pr.astype(mv.dtype), mv)


def conv_ffn(h, w_up, conv_w, conv_b, w_down):
    u = h @ w_up
    up = jnp.pad(u, ((0, 0), (1, 1), (0, 0)))
    u = up[:, :-2] * conv_w[0] + up[:, 1:-1] * conv_w[1] + up[:, 2:] * conv_w[2] + conv_b
    a, b = u[..., :D_FF], u[..., D_FF:]
    return (jax.nn.silu(a) * b) @ w_down


def trunk(x, mem, norm_mix, norm_mem, w_mem_kv, xq_norm, xk_norm, a_w_in, a_w_out, a_q_norm, a_k_norm,
          a_lambda, a_subln, b_w_in, b_w_out, b_q_norm, b_k_norm, norm_ffn, w_up, conv_w, conv_b, w_down):
    B, S, _ = x.shape
    M = mem.shape[1]
    cos_a, sin_a = rope_tables(S, A_HEAD_DIM)
    cos_b, sin_b = rope_tables(S, B_HEAD_DIM)
    for i in range(DEPTH):
        j = i // 2
        h = rms_norm(x, norm_mix[i])
        kv = (rms_norm(mem, norm_mem[i]) @ w_mem_kv[i]).reshape(B, M, 2, X_HEADS, X_HEAD_DIM)
        mk = rms_norm(kv[:, :, 0], xk_norm[i])
        mv = kv[:, :, 1]
        if i % 2 == 0:
            p = h @ a_w_in[j]
            lam_init = 0.8 - 0.6 * math.exp(-0.3 * i)
            mixed = diff_mixer(p[..., :-X_WIDTH], cos_a, sin_a, a_q_norm[j], a_k_norm[j],
                               a_lambda[j], a_subln[j], lam_init)
            w_out = a_w_out[j]
        else:
            p = h @ b_w_in[j]
            mixed = dilated_mixer(p[..., :-X_WIDTH], cos_b, sin_b, b_q_norm[j], b_k_norm[j])
            w_out = b_w_out[j]
        xq = rms_norm(p[..., -X_WIDTH:].reshape(B, S, X_HEADS, X_HEAD_DIM), xq_norm[i])
        cross = memory_attention(xq, mk, mv).reshape(B, S, X_WIDTH)
        x = x + jnp.concatenate([mixed, cross], axis=-1) @ w_out
        x = x + conv_ffn(rms_norm(x, norm_ffn[i]), w_up[i], conv_w[i], conv_b[i], w_down[i])
    return x


def setup_inputs(seed: int = 0) -> dict:
    key = jax.random.key(seed)
    keys = iter(jax.random.split(key, 32))
    f32 = jnp.float32

    def nrm(shape, scale):
        return jax.random.normal(next(keys), shape, f32) * scale

    def gain(shape):
        return 1.0 + nrm(shape, 0.1)

    return {
        'x_prompt': nrm((BATCH, SEQ, D_MODEL), 1.0),
        'x_sample': nrm((DEC_BATCH, DEC_SEQ, D_MODEL), 1.0),
        'mem_prompt': nrm((BATCH, N_MEM, D_MODEL), 1.0),
        'mem_sample': nrm((DEC_BATCH, N_MEM, D_MODEL), 1.0),
        'norm_mix': gain((DEPTH, D_MODEL)),
        'norm_mem': gain((DEPTH, D_MODEL)),
        'w_mem_kv': nrm((DEPTH, D_MODEL, 2 * X_WIDTH), D_MODEL ** -0.5),
        'xq_norm': gain((DEPTH, X_HEAD_DIM)),
        'xk_norm': gain((DEPTH, X_HEAD_DIM)),
        'a_w_in': nrm((N_A_LAYERS, D_MODEL, A_IN), D_MODEL ** -0.5),
        'a_w_out': nrm((N_A_LAYERS, A_OUT, D_MODEL), A_OUT ** -0.5),
        'a_q_norm': gain((N_A_LAYERS, A_HEAD_DIM)),
        'a_k_norm': gain((N_A_LAYERS, A_HEAD_DIM)),
        'a_lambda': nrm((N_A_LAYERS, 4, A_HEAD_DIM), 0.1),
        'a_subln': gain((N_A_LAYERS, 2 * A_HEAD_DIM)),
        'b_w_in': nrm((N_B_LAYERS, D_MODEL, B_IN), D_MODEL ** -0.5),
        'b_w_out': nrm((N_B_LAYERS, B_OUT, D_MODEL), B_OUT ** -0.5),
        'b_q_norm': gain((N_B_LAYERS, N_GROUPS, B_HEAD_DIM)),
        'b_k_norm': gain((N_B_LAYERS, N_GROUPS, B_HEAD_DIM)),
        'norm_ffn': gain((DEPTH, D_MODEL)),
        'w_up': nrm((DEPTH, D_MODEL, 2 * D_FF), D_MODEL ** -0.5),
        'conv_w': jnp.array([0.0, 1.0, 0.0], f32)[None, :, None] + nrm((DEPTH, CONV_WIDTH, 2 * D_FF), 0.3),
        'conv_b': nrm((DEPTH, 2 * D_FF), 0.02),
        'w_down': nrm((DEPTH, D_FF, D_MODEL), D_FF ** -0.5),
    }


def reference(x_prompt, x_sample, mem_prompt, mem_sample, norm_mix, norm_mem, w_mem_kv, xq_norm, xk_norm,
              a_w_in, a_w_out, a_q_norm, a_k_norm, a_lambda, a_subln, b_w_in, b_w_out, b_q_norm, b_k_norm,
              norm_ffn, w_up, conv_w, conv_b, w_down):
    y_prompt = trunk(x_prompt, mem_prompt, norm_mix, norm_mem, w_mem_kv, xq_norm, xk_norm, a_w_in, a_w_out,
                     a_q_norm, a_k_norm, a_lambda, a_subln, b_w_in, b_w_out, b_q_norm, b_k_norm,
                     norm_ffn, w_up, conv_w, conv_b, w_down)
    y_sample = trunk(x_sample, mem_sample, norm_mix, norm_mem, w_mem_kv, xq_norm, xk_norm, a_w_in, a_w_out,
                     a_q_norm, a_k_norm, a_lambda, a_subln, b_w_in, b_w_out, b_q_norm, b_k_norm,
                     norm_ffn, w_up, conv_w, conv_b, w_down)
    return (y_prompt, y_sample)
```

```python
import functools
import math

import jax
import jax.numpy as jnp
from jax import lax
from jax.experimental import pallas as pl
from jax.experimental.pallas import tpu as pltpu

F32 = jnp.float32
BF16 = jnp.bfloat16

LANES = 128
HEAD_DIM = 64
ROT_HALF = 8
A_HEADS = 8
B_HEADS = 8
B_GROUPS = ((128, 1), (512, 4), (2048, 16))
X_HEADS = 4
D_FF = 2816
ROPE_THETA = 500000.0
EPS = 1e-6
NEG_INF = -1e30
LOG2E = math.log2(math.e)
Q_SCALE = LOG2E / math.sqrt(HEAD_DIM)

V7X_VMEM_BYTES = 64 * 1024 * 1024
VMEM_LIMIT = V7X_VMEM_BYTES * 7 // 8

ROW_TILE = 512
Q_TILE_A = 256
Q_TILE_B = 128
MXU_COLS = 256


def _params(*sem):
    return pltpu.CompilerParams(dimension_semantics=sem, vmem_limit_bytes=VMEM_LIMIT)


def _full(shape):
    return pl.BlockSpec(shape, lambda *_: (0,) * len(shape))


def _rms_rows(x, g):
    ms = jnp.mean(x * x, axis=-1, keepdims=True)
    return x * lax.rsqrt(ms + EPS) * g


def _pair_norm(y, group_ones, gain):
    ssq = jnp.dot((y * y).astype(BF16), group_ones, preferred_element_type=F32)
    return y * lax.rsqrt(ssq * (1.0 / HEAD_DIM) + EPS) * gain


def _rope(y, cos, sin_lo, sin_hi):
    return (y * cos + pltpu.roll(y, ROT_HALF, axis=1) * sin_lo
            + pltpu.roll(y, LANES - ROT_HALF, axis=1) * sin_hi)


def _split_heads(q):
    lane = lax.broadcasted_iota(jnp.int32, q.shape, 1)
    zero = jnp.zeros_like(q)
    return jnp.where(lane < HEAD_DIM, q, zero), jnp.where(lane >= HEAD_DIM, q, zero)


def _merge_heads(o_lo, o_hi):
    lane = lax.broadcasted_iota(jnp.int32, o_lo.shape, 1)
    return jnp.where(lane < HEAD_DIM, o_lo, o_hi)


def _scores(q, k):
    return lax.dot_general(q, k, (((1,), (1,)), ((), ())), preferred_element_type=F32)


def _inproj_kernel(nq, nk, nv, nx, x_ref, g_ref, w_ref, ones_ref, cos_ref, slo_ref, shi_ref,
                   qg_ref, kg_ref, xg_ref, q_ref, k_ref, v_ref, xq_ref, h_ref):
    h_ref[...] = _rms_rows(x_ref[0], g_ref[...]).astype(BF16)
    ones = ones_ref[...]
    cos, slo, shi = cos_ref[...], slo_ref[...], shi_ref[...]
    pairs_per_chunk = MXU_COLS // LANES
    for ch in range((nq + nk + nv + nx) // pairs_per_chunk):
        acc = jnp.dot(h_ref[...], w_ref[:, ch * MXU_COLS:(ch + 1) * MXU_COLS],
                      preferred_element_type=F32)
        for half in range(pairs_per_chunk):
            p = ch * pairs_per_chunk + half
            y = acc[:, half * LANES:(half + 1) * LANES]
            if p < nq:
                y = _rope(_pair_norm(y, ones, qg_ref[p:p + 1, :]), cos, slo, shi)
                q_ref[0, p] = y.astype(BF16)
            elif p < nq + nk:
                p -= nq
                y = _rope(_pair_norm(y, ones, kg_ref[p:p + 1, :]), cos, slo, shi)
                k_ref[0, p] = y.astype(BF16)
            elif p < nq + nk + nv:
                v_ref[0, p - nq - nk] = y.astype(BF16)
            else:
                y = _pair_norm(y, ones, xg_ref[...])
                xq_ref[0, p - nq - nk - nv] = y.astype(BF16)


def _inproj(x, g, w, ones, rope, qg, kg, xg, nv):
    B, S, D = x.shape
    nq, nk, nx = qg.shape[0], kg.shape[0], X_HEADS // 2
    tm = ROW_TILE
    row = lambda shape: pl.BlockSpec(shape, lambda b, i: (b, 0, i, 0))
    tab = pl.BlockSpec((tm, LANES), lambda b, i: (i, 0))
    out = lambda n: jax.ShapeDtypeStruct((B, n, S, LANES), BF16)
    return pl.pallas_call(
        functools.partial(_inproj_kernel, nq, nk, nv, nx),
        grid=(B, S // tm),
        in_specs=[pl.BlockSpec((1, tm, D), lambda b, i: (b, i, 0)), _full(g.shape), _full(w.shape),
                  _full(ones.shape), tab, tab, tab, _full(qg.shape), _full(kg.shape), _full(xg.shape)],
        out_specs=[row((1, nq, tm, LANES)), row((1, nk, tm, LANES)), row((1, nv, tm, LANES)),
                   row((1, nx, tm, LANES))],
        out_shape=[out(nq), out(nk), out(nv), out(nx)],
        scratch_shapes=[pltpu.VMEM((tm, D), BF16)],
        compiler_params=_params("parallel", "parallel"),
        name="inproj",
    )(x, g, w, ones, *rope, qg, kg, xg)


def _softmax_pv(q, k, v):
    s = _scores(q, k)
    m = jnp.max(s, axis=-1, keepdims=True)
    e = jnp.exp2(s - m)
    l = jnp.sum(e, axis=-1, keepdims=True)
    return jnp.dot(e.astype(BF16), v, preferred_element_type=F32), l


def _diff_attn_kernel(lam_init, q_ref, k_ref, v_ref, lam_ref, sg_ref, o_ref):
    k, v = k_ref[0, 0], v_ref[0, 0]
    q1, q2 = _split_heads(q_ref[0, 0])
    lp = lam_ref[...]
    lam = (jnp.exp(jnp.sum(lp[0:1] * lp[1:2], axis=-1, keepdims=True))
           - jnp.exp(jnp.sum(lp[2:3] * lp[3:4], axis=-1, keepdims=True)) + lam_init)
    o1, l1 = _softmax_pv(q1, k, v)
    o2, l2 = _softmax_pv(q2, k, v)
    o = o1 / l1 - lam * (o2 / l2)
    o_ref[0] = (_rms_rows(o, sg_ref[...])).astype(BF16)


def _diff_attn(q, k, v, lam_p, subln, lam_init):
    B, H, S, _ = q.shape
    tq = Q_TILE_A
    kv = pl.BlockSpec((1, 1, S, LANES), lambda b, h, i: (b, h, 0, 0))
    return pl.pallas_call(
        functools.partial(_diff_attn_kernel, lam_init),
        grid=(B, H, S // tq),
        in_specs=[pl.BlockSpec((1, 1, tq, LANES), lambda b, h, i: (b, h, i, 0)), kv, kv,
                  _full(lam_p.shape), _full(subln.shape)],
        out_specs=pl.BlockSpec((1, tq, LANES), lambda b, h, i: (b, i, h)),
        out_shape=jax.ShapeDtypeStruct((B, S, H * LANES), BF16),
        compiler_params=_params("parallel", "parallel", "parallel"),
        name="diff_attn",
    )(q, k, v, lam_p, subln)


def _dilated_attn_kernel(seq, q_ref, k_ref, v_ref, o_ref):
    tq = q_ref.shape[3]
    q0 = pl.program_id(2) * tq
    parts = []
    for g, (window, dil) in enumerate(B_GROUPS):
        reach = window // 2
        span = min(seq, tq + 2 * reach)
        start = pl.multiple_of(jnp.clip(q0 - reach, 0, seq - span), tq // 2)
        kpos = start + lax.broadcasted_iota(jnp.int32, (tq, span), 1)
        qpos = q0 + lax.broadcasted_iota(jnp.int32, (tq, span), 0)
        rel = kpos - qpos
        valid = (jnp.abs(rel) <= reach) & ((rel & (dil - 1)) == 0)
        parts.append((_split_heads(q_ref[0, g, 0]), k_ref[0, g, 0, pl.ds(start, span), :],
                      v_ref[0, g, 0, pl.ds(start, span), :], valid))
    outs = []
    for head in range(2):
        s = [jnp.where(valid, _scores(qs[head], kw), NEG_INF) for qs, kw, _, valid in parts]
        m = functools.reduce(jnp.maximum, [jnp.max(x, axis=-1, keepdims=True) for x in s])
        e = [jnp.exp2(x - m) for x in s]
        z = functools.reduce(jnp.add, [jnp.sum(x, axis=-1, keepdims=True) for x in e])
        o = functools.reduce(jnp.add, [jnp.dot(x.astype(BF16), part[2], preferred_element_type=F32)
                                       for x, part in zip(e, parts)])
        outs.append(o / z)
    o_ref[0] = _merge_heads(*outs).astype(BF16)


def _dilated_attn(q, k, v):
    B, _, S, _ = q.shape
    G, P = len(B_GROUPS), B_HEADS // 2
    q, k, v = (t.reshape(B, G, P, S, LANES) for t in (q, k, v))
    tq = Q_TILE_B
    kv = pl.BlockSpec((1, G, 1, S, LANES), lambda b, p, i: (b, 0, p, 0, 0))
    return pl.pallas_call(
        functools.partial(_dilated_attn_kernel, S),
        grid=(B, P, S // tq),
        in_specs=[pl.BlockSpec((1, G, 1, tq, LANES), lambda b, p, i: (b, 0, p, i, 0)), kv, kv],
        out_specs=pl.BlockSpec((1, tq, LANES), lambda b, p, i: (b, i, p)),
        out_shape=jax.ShapeDtypeStruct((B, S, P * LANES), BF16),
        compiler_params=_params("parallel", "parallel", "parallel"),
        name="dilated_attn",
    )(q, k, v)


def _mem_kv_kernel(m_ref, g_ref, w_ref, ones_ref, kg_ref, k_ref, v_ref):
    h = _rms_rows(m_ref[0], g_ref[...]).astype(BF16)
    kv = jnp.dot(h, w_ref[...], preferred_element_type=F32)
    half = kv.shape[1] // 2
    for p in range(half // LANES):
        y = _pair_norm(kv[:, p * LANES:(p + 1) * LANES], ones_ref[...], kg_ref[...])
        k_ref[0, :, p * LANES:(p + 1) * LANES] = y.astype(BF16)
    v_ref[0] = kv[:, half:].astype(BF16)


def _mem_kv(mem, g, w, ones, kg):
    B, M, D = mem.shape
    width = w.shape[1] // 2
    blk = pl.BlockSpec((1, M, width), lambda b: (b, 0, 0))
    return pl.pallas_call(
        _mem_kv_kernel,
        grid=(B,),
        in_specs=[pl.BlockSpec((1, M, D), lambda b: (b, 0, 0)), _full(g.shape), _full(w.shape),
                  _full(ones.shape), _full(kg.shape)],
        out_specs=[blk, blk],
        out_shape=[jax.ShapeDtypeStruct((B, M, width), BF16)] * 2,
        compiler_params=_params("parallel"),
        name="mem_kv",
    )(mem, g, w, ones, kg)


def _cross_attn_kernel(q_ref, k_ref, v_ref, o_ref):
    for p in range(q_ref.shape[1]):
        cols = slice(p * LANES, (p + 1) * LANES)
        k, v = k_ref[0, :, cols], v_ref[0, :, cols]
        outs = []
        for qh in _split_heads(q_ref[0, p]):
            o, l = _softmax_pv(qh, k, v)
            outs.append(o / l)
        o_ref[0, :, cols] = _merge_heads(*outs).astype(BF16)


def _cross_attn(xq, mk, mv):
    B, P, S, _ = xq.shape
    M, width = mk.shape[1:]
    ts = ROW_TILE
    kv = pl.BlockSpec((1, M, width), lambda b, i: (b, 0, 0))
    return pl.pallas_call(
        _cross_attn_kernel,
        grid=(B, S // ts),
        in_specs=[pl.BlockSpec((1, P, ts, LANES), lambda b, i: (b, 0, i, 0)), kv, kv],
        out_specs=pl.BlockSpec((1, ts, width), lambda b, i: (b, i, 0)),
        out_shape=jax.ShapeDtypeStruct((B, S, width), BF16),
        compiler_params=_params("parallel", "parallel"),
        name="cross_attn",
    )(xq, mk, mv)


def _outproj_kernel(mix_ref, cross_ref, x_ref, w_ref, o_ref):
    cm = mix_ref.shape[2]
    o_ref[0] = (x_ref[0]
                + jnp.dot(mix_ref[0], w_ref[:cm, :], preferred_element_type=F32)
                + jnp.dot(cross_ref[0], w_ref[cm:, :], preferred_element_type=F32))


def _outproj(mixed, cross, x, w):
    B, S, D = x.shape
    tm = ROW_TILE
    row = lambda c: pl.BlockSpec((1, tm, c), lambda b, i: (b, i, 0))
    return pl.pallas_call(
        _outproj_kernel,
        grid=(B, S // tm),
        in_specs=[row(mixed.shape[2]), row(cross.shape[2]), row(D), _full(w.shape)],
        out_specs=row(D),
        out_shape=jax.ShapeDtypeStruct((B, S, D), F32),
        compiler_params=_params("parallel", "parallel"),
        name="outproj",
    )(mixed, cross, x, w)


FF_CHUNK = 256


def _ffn_up_kernel(x_ref, g_ref, w_ref, u_ref, h_ref):
    h_ref[...] = _rms_rows(x_ref[0], g_ref[...]).astype(BF16)
    for c in range(w_ref.shape[1] // FF_CHUNK):
        cols = slice(c * FF_CHUNK, (c + 1) * FF_CHUNK)
        u_ref[0, :, cols] = jnp.dot(h_ref[...], w_ref[:, cols], preferred_element_type=F32).astype(BF16)


def _ffn_up(x, g, w):
    B, S, D = x.shape
    N = w.shape[1]
    tm = ROW_TILE
    return pl.pallas_call(
        _ffn_up_kernel,
        grid=(B, S // tm),
        in_specs=[pl.BlockSpec((1, tm, D), lambda b, i: (b, i, 0)), _full(g.shape), _full(w.shape)],
        out_specs=pl.BlockSpec((1, tm, N), lambda b, i: (b, i, 0)),
        out_shape=jax.ShapeDtypeStruct((B, S, N), BF16),
        scratch_shapes=[pltpu.VMEM((tm, D), BF16)],
        compiler_params=_params("parallel", "parallel"),
        name="ffn_up",
    )(x, g, w)


HALO_ROWS = 16


def _ffn_down_kernel(u_ref, prev_ref, next_ref, cw_ref, cb_ref, x_ref, w_ref, o_ref, g_ref):
    tm = u_ref.shape[1]
    i, n = pl.program_id(1), pl.num_programs(1)
    has_prev = (i > 0).astype(F32)
    has_next = (i < n - 1).astype(F32)
    row = lax.broadcasted_iota(jnp.int32, (tm, FF_CHUNK), 0)

    def conv(c0):
        cols = slice(c0, c0 + FF_CHUNK)
        u = u_ref[0, :, cols].astype(F32)
        before = prev_ref[0, :, cols].astype(F32)[HALO_ROWS - 1:HALO_ROWS] * has_prev
        after = next_ref[0, :, cols].astype(F32)[0:1] * has_next
        up = jnp.where(row == 0, before, pltpu.roll(u, 1, axis=0))
        dn = jnp.where(row == tm - 1, after, pltpu.roll(u, tm - 1, axis=0))
        return (up * cw_ref[0:1, cols] + u * cw_ref[1:2, cols] + dn * cw_ref[2:3, cols]
                + cb_ref[:, cols])

    for c0 in range(0, D_FF, FF_CHUNK):
        a, b = conv(c0), conv(D_FF + c0)
        g_ref[:, c0:c0 + FF_CHUNK] = (a / (1.0 + jnp.exp(-a)) * b).astype(BF16)
    o_ref[0] = x_ref[0] + jnp.dot(g_ref[...], w_ref[...], preferred_element_type=F32)


def _ffn_down(u, conv_w, conv_b, x, w):
    B, S, D = x.shape
    N = u.shape[2]
    tm = ROW_TILE
    halo_per_tile = tm // HALO_ROWS
    last_halo = S // HALO_ROWS - 1
    return pl.pallas_call(
        _ffn_down_kernel,
        grid=(B, S // tm),
        in_specs=[pl.BlockSpec((1, tm, N), lambda b, i: (b, i, 0)),
                  pl.BlockSpec((1, HALO_ROWS, N), lambda b, i: (b, jnp.maximum(i * halo_per_tile - 1, 0), 0)),
                  pl.BlockSpec((1, HALO_ROWS, N),
                               lambda b, i: (b, jnp.minimum((i + 1) * halo_per_tile, last_halo), 0)),
                  _full(conv_w.shape), _full(conv_b.shape),
                  pl.BlockSpec((1, tm, D), lambda b, i: (b, i, 0)), _full(w.shape)],
        out_specs=pl.BlockSpec((1, tm, D), lambda b, i: (b, i, 0)),
        out_shape=jax.ShapeDtypeStruct((B, S, D), F32),
        scratch_shapes=[pltpu.VMEM((tm, D_FF), BF16)],
        compiler_params=_params("parallel", "parallel"),
        name="ffn_down",
    )(u, u, u, conv_w, conv_b, x, w)


def _rope_tables(S):
    inv = ROPE_THETA ** (-(jnp.arange(ROT_HALF, dtype=F32) * 2.0 / (2 * ROT_HALF)))
    ang = jnp.arange(S, dtype=F32)[:, None] * inv[None, :]
    cos, sin = jnp.cos(ang), jnp.sin(ang)
    rest = HEAD_DIM - 2 * ROT_HALF
    one, zero = jnp.ones((S, rest), F32), jnp.zeros((S, rest), F32)
    zhalf = jnp.zeros((S, ROT_HALF), F32)
    per_head = (jnp.concatenate([cos, cos, one], axis=1),
                jnp.concatenate([zhalf, sin, zero], axis=1),
                jnp.concatenate([-sin, zhalf, zero], axis=1))
    return tuple(jnp.tile(t, (1, LANES // HEAD_DIM)) for t in per_head)


def _pair_gain(g, scale=1.0):
    g = jnp.atleast_2d(g.astype(F32) * scale)
    return jnp.tile(g, (1, LANES // HEAD_DIM))


def _group_ones():
    head = jnp.arange(LANES) // HEAD_DIM
    return (head[:, None] == head[None, :]).astype(BF16)


def _a_weight(w):
    d = w.shape[0]
    n = A_HEADS * HEAD_DIM
    pair = lambda t: t.reshape(d, 2, A_HEADS, HEAD_DIM).transpose(0, 2, 1, 3).reshape(d, 2 * n)
    return jnp.concatenate([pair(w[:, :2 * n]), pair(w[:, 2 * n:4 * n]), w[:, 4 * n:]], axis=1).astype(BF16)


def _trunk(x, mem, P):
    S = x.shape[1]
    rope = _rope_tables(S)
    ones = _group_ones()
    n_layers = P["norm_mix"].shape[0]
    for i in range(n_layers):
        j = i // 2
        row = lambda name: P[name][i][None, :].astype(F32)
        mk, mv = _mem_kv(mem, row("norm_mem"), P["w_mem_kv"][i].astype(BF16), ones,
                         _pair_gain(P["xk_norm"][i]))
        xg = _pair_gain(P["xq_norm"][i], Q_SCALE)
        if i % 2 == 0:
            qg = jnp.tile(_pair_gain(P["a_q_norm"][j], Q_SCALE), (A_HEADS, 1))
            kg = jnp.tile(_pair_gain(P["a_k_norm"][j]), (A_HEADS, 1))
            q, k, v, xq = _inproj(x, row("norm_mix"), _a_weight(P["a_w_in"][j]), ones, rope,
                                  qg, kg, xg, A_HEADS)
            lam_init = 0.8 - 0.6 * math.exp(-0.3 * i)
            subln = P["a_subln"][j][None, :].astype(F32) * (1.0 - lam_init)
            mixed = _diff_attn(q, k, v, P["a_lambda"][j].astype(F32), subln, lam_init)
            w_out = P["a_w_out"][j]
        else:
            pairs = B_HEADS // 2
            qg = jnp.repeat(_pair_gain(P["b_q_norm"][j], Q_SCALE), pairs, axis=0)
            kg = jnp.repeat(_pair_gain(P["b_k_norm"][j]), pairs, axis=0)
            q, k, v, xq = _inproj(x, row("norm_mix"), P["b_w_in"][j].astype(BF16), ones, rope,
                                  qg, kg, xg, len(B_GROUPS) * pairs)
            mixed = _dilated_attn(q, k, v)
            w_out = P["b_w_out"][j]
        cross = _cross_attn(xq, mk, mv)
        x = _outproj(mixed, cross, x, w_out.astype(BF16))
        u = _ffn_up(x, row("norm_ffn"), P["w_up"][i].astype(BF16))
        x = _ffn_down(u, P["conv_w"][i].astype(F32), P["conv_b"][i][None, :].astype(F32), x,
                      P["w_down"][i].astype(BF16))
    return x


def kernel(x_prompt, x_sample, mem_prompt, mem_sample, norm_mix, norm_mem, w_mem_kv, xq_norm, xk_norm, a_w_in, a_w_out, a_q_norm, a_k_norm, a_lambda, a_subln, b_w_in, b_w_out, b_q_norm, b_k_norm, norm_ffn, w_up, conv_w, conv_b, w_down):
    P = dict(norm_mix=norm_mix, norm_mem=norm_mem, w_mem_kv=w_mem_kv, xq_norm=xq_norm, xk_norm=xk_norm,
             a_w_in=a_w_in, a_w_out=a_w_out, a_q_norm=a_q_norm, a_k_norm=a_k_norm, a_lambda=a_lambda,
             a_subln=a_subln, b_w_in=b_w_in, b_w_out=b_w_out, b_q_norm=b_q_norm, b_k_norm=b_k_norm,
             norm_ffn=norm_ffn, w_up=w_up, conv_w=conv_w, conv_b=conv_b, w_down=w_down)
    return _trunk(x_prompt, mem_prompt, P), _trunk(x_sample, mem_sample, P)
```

```python
import functools
import math

import jax
import jax.numpy as jnp
from jax import lax
from jax.experimental import pallas as pl
from jax.experimental.pallas import tpu as pltpu

F32 = jnp.float32
BF16 = jnp.bfloat16

LANES = 128
HEAD_DIM = 64
ROT_HALF = 8
A_HEADS = 8
B_HEADS = 8
B_GROUPS = ((128, 1), (512, 4), (2048, 16))
X_HEADS = 4
D_FF = 2816
ROPE_THETA = 500000.0
EPS = 1e-6
NEG_INF = -1e30
LOG2E = math.log2(math.e)
Q_SCALE = LOG2E / math.sqrt(HEAD_DIM)

V7X_VMEM_BYTES = 64 * 1024 * 1024
VMEM_LIMIT = V7X_VMEM_BYTES * 7 // 8

ROW_TILE = 512
Q_TILE_A = 256
Q_TILE_B = 128
MXU_COLS = 256


def _params(*sem):
    return pltpu.CompilerParams(dimension_semantics=sem, vmem_limit_bytes=VMEM_LIMIT)


def _full(shape):
    return pl.BlockSpec(shape, lambda *_: (0,) * len(shape))


def _rms_rows(x, g):
    ms = jnp.mean(x * x, axis=-1, keepdims=True)
    return x * lax.rsqrt(ms + EPS) * g


def _pair_norm(y, group_ones, gain):
    ssq = jnp.dot((y * y).astype(BF16), group_ones, preferred_element_type=F32)
    return y * lax.rsqrt(ssq * (1.0 / HEAD_DIM) + EPS) * gain


def _rope(y, cos, sin_lo, sin_hi):
    return (y * cos + pltpu.roll(y, ROT_HALF, axis=1) * sin_lo
            + pltpu.roll(y, LANES - ROT_HALF, axis=1) * sin_hi)


def _split_heads(q):
    lane = lax.broadcasted_iota(jnp.int32, q.shape, 1)
    zero = jnp.zeros_like(q)
    return jnp.where(lane < HEAD_DIM, q, zero), jnp.where(lane >= HEAD_DIM, q, zero)


def _merge_heads(o_lo, o_hi):
    lane = lax.broadcasted_iota(jnp.int32, o_lo.shape, 1)
    return jnp.where(lane < HEAD_DIM, o_lo, o_hi)


def _scores(q, k):
    return lax.dot_general(q, k, (((1,), (1,)), ((), ())), preferred_element_type=F32)


def _inproj_kernel(nq, nk, nv, nx, x_ref, g_ref, w_ref, ones_ref, cos_ref, slo_ref, shi_ref,
                   qg_ref, kg_ref, xg_ref, *out_and_scratch):
    h_ref = out_and_scratch[-1]
    q_ref, k_ref, v_ref = out_and_scratch[:3]
    tl, d_model = x_ref.shape[1], h_ref.shape[1]
    res = x_ref.shape[2] // d_model
    for r in range(res):
        h_ref[r * tl:(r + 1) * tl, :] = _rms_rows(x_ref[0, :, r * d_model:(r + 1) * d_model],
                                                  g_ref[...]).astype(BF16)

    def stacked(ref):
        return jnp.concatenate([ref[:, r * LANES:(r + 1) * LANES] for r in range(res)], axis=0)

    def put(ref, p, y):
        for r in range(res):
            ref[0, r, p] = y[r * tl:(r + 1) * tl].astype(BF16)

    ones = ones_ref[...]
    cos, slo, shi = stacked(cos_ref), stacked(slo_ref), stacked(shi_ref)
    pairs_per_chunk = MXU_COLS // LANES
    for ch in range((nq + nk + nv + nx) // pairs_per_chunk):
        acc = jnp.dot(h_ref[...], w_ref[:, ch * MXU_COLS:(ch + 1) * MXU_COLS],
                      preferred_element_type=F32)
        for half in range(pairs_per_chunk):
            p = ch * pairs_per_chunk + half
            y = acc[:, half * LANES:(half + 1) * LANES]
            if p < nq:
                put(q_ref, p, _rope(_pair_norm(y, ones, qg_ref[p:p + 1, :]), cos, slo, shi))
            elif p < nq + nk:
                p -= nq
                put(k_ref, p, _rope(_pair_norm(y, ones, kg_ref[p:p + 1, :]), cos, slo, shi))
            elif p < nq + nk + nv:
                put(v_ref, p - nq - nk, y)
            else:
                put(out_and_scratch[3], p - nq - nk - nv, _pair_norm(y, ones, xg_ref[...]))


def _inproj(x, g, w, ones, rope, qg, kg, xg, nv, dil=1, with_xq=True):
    B, S, D = x.shape
    L = S // dil
    nq, nk, nx = qg.shape[0], kg.shape[0], (X_HEADS // 2 if with_xq else 0)
    tl = min(L, ROW_TILE)
    res = ROW_TILE // tl
    x = x.reshape(B, L, dil * D)
    rope = [t.reshape(L, dil * LANES) for t in rope]
    tab = pl.BlockSpec((tl, res * LANES), lambda b, r, i: (i, r))
    sizes = [nq, nk, nv] + ([nx] if with_xq else [])
    return pl.pallas_call(
        functools.partial(_inproj_kernel, nq, nk, nv, nx),
        grid=(B, dil // res, L // tl),
        in_specs=[pl.BlockSpec((1, tl, res * D), lambda b, r, i: (b, i, r)), _full(g.shape), _full(w.shape),
                  _full(ones.shape), tab, tab, tab, _full(qg.shape), _full(kg.shape), _full(xg.shape)],
        out_specs=[pl.BlockSpec((1, res, n, tl, LANES), lambda b, r, i: (b, r, 0, i, 0)) for n in sizes],
        out_shape=[jax.ShapeDtypeStruct((B, dil, n, L, LANES), BF16) for n in sizes],
        scratch_shapes=[pltpu.VMEM((res * tl, D), BF16)],
        compiler_params=_params("parallel", "parallel", "parallel"),
        name="inproj",
    )(x, g, w, ones, *rope, qg, kg, xg)


def _softmax_pv(q, k, v):
    s = _scores(q, k)
    m = jnp.max(s, axis=-1, keepdims=True)
    e = jnp.exp2(s - m)
    l = jnp.sum(e, axis=-1, keepdims=True)
    return jnp.dot(e.astype(BF16), v, preferred_element_type=F32), l


def _diff_attn_kernel(lam_init, q_ref, k_ref, v_ref, lam_ref, sg_ref, o_ref):
    k, v = k_ref[0, 0], v_ref[0, 0]
    q1, q2 = _split_heads(q_ref[0, 0])
    lp = lam_ref[...]
    lam = (jnp.exp(jnp.sum(lp[0:1] * lp[1:2], axis=-1, keepdims=True))
           - jnp.exp(jnp.sum(lp[2:3] * lp[3:4], axis=-1, keepdims=True)) + lam_init)
    o1, l1 = _softmax_pv(q1, k, v)
    o2, l2 = _softmax_pv(q2, k, v)
    o = o1 / l1 - lam * (o2 / l2)
    o_ref[0] = (_rms_rows(o, sg_ref[...])).astype(BF16)


def _diff_attn(q, k, v, lam_p, subln, lam_init):
    B, H, S, _ = q.shape
    tq = Q_TILE_A
    kv = pl.BlockSpec((1, 1, S, LANES), lambda b, h, i: (b, h, 0, 0))
    return pl.pallas_call(
        functools.partial(_diff_attn_kernel, lam_init),
        grid=(B, H, S // tq),
        in_specs=[pl.BlockSpec((1, 1, tq, LANES), lambda b, h, i: (b, h, i, 0)), kv, kv,
                  _full(lam_p.shape), _full(subln.shape)],
        out_specs=pl.BlockSpec((1, tq, LANES), lambda b, h, i: (b, i, h)),
        out_shape=jax.ShapeDtypeStruct((B, S, H * LANES), BF16),
        compiler_params=_params("parallel", "parallel", "parallel"),
        name="diff_attn",
    )(q, k, v, lam_p, subln)


def _band_attn_kernel(radius, q_ref, k_ref, v_ref, o_ref, l_ref):
    _, res, pairs, rows, _ = q_ref.shape
    L = k_ref.shape[3]
    tq = min(Q_TILE_B, L)
    span = min(L, tq + 2 * radius)
    row0 = pl.program_id(2) * rows
    col = lax.broadcasted_iota(jnp.int32, (tq, span), 1)
    qrow = lax.broadcasted_iota(jnp.int32, (tq, span), 0)
    for r in range(res):
        for blk in range(rows // tq):
            q0 = row0 + blk * tq
            start = pl.multiple_of(jnp.clip(q0 - radius, 0, L - span), radius)
            valid = jnp.abs(col - qrow + (start - q0)) <= radius
            rs = slice(blk * tq, (blk + 1) * tq)
            for p in range(pairs):
                kw = k_ref[0, r, p, pl.ds(start, span), :]
                vw = v_ref[0, r, p, pl.ds(start, span), :]
                outs, lses = [], []
                for qh in _split_heads(q_ref[0, r, p, rs, :]):
                    s = jnp.where(valid, _scores(qh, kw), NEG_INF)
                    m = jnp.max(s, axis=-1, keepdims=True)
                    e = jnp.exp2(s - m)
                    z = jnp.sum(e, axis=-1, keepdims=True)
                    outs.append(jnp.dot(e.astype(BF16), vw, preferred_element_type=F32) / z)
                    lses.append(jnp.broadcast_to(m + jnp.log2(z), (tq, LANES)))
                o_ref[0, r, p, rs, :] = _merge_heads(*outs).astype(BF16)
                l_ref[0, r, p, rs, :] = _merge_heads(*lses)


def _band_attn(q, k, v, radius):
    B, dil, pairs, L, _ = q.shape
    rows = min(L, ROW_TILE)
    res = ROW_TILE // rows
    kv = pl.BlockSpec((1, res, pairs, L, LANES), lambda b, r, i: (b, r, 0, 0, 0))
    tile = pl.BlockSpec((1, res, pairs, rows, LANES), lambda b, r, i: (b, r, 0, i, 0))
    return pl.pallas_call(
        functools.partial(_band_attn_kernel, radius),
        grid=(B, dil // res, L // rows),
        in_specs=[tile, kv, kv],
        out_specs=[tile, tile],
        out_shape=[jax.ShapeDtypeStruct(q.shape, BF16), jax.ShapeDtypeStruct(q.shape, F32)],
        compiler_params=_params("parallel", "parallel", "parallel"),
        name="band_attn",
    )(q, k, v)


def _group_mix_kernel(*refs):
    n = (len(refs) - 1) // 2
    o_refs, l_refs, out_ref = refs[:n], refs[n:2 * n], refs[-1]
    for p in range(o_refs[0].shape[2]):
        ls = [l[0, 0, p] for l in l_refs]
        m = functools.reduce(jnp.maximum, ls)
        ws = [jnp.exp2(l - m) for l in ls]
        num = functools.reduce(jnp.add, [w * o[0, 0, p].astype(F32) for w, o in zip(ws, o_refs)])
        out_ref[0, :, p * LANES:(p + 1) * LANES] = (num / functools.reduce(jnp.add, ws)).astype(BF16)


def _group_mix(outs, lses):
    B, _, pairs, S, _ = outs[0].shape
    top = max(d for _, d in B_GROUPS)
    rows = S // top

    def view(t):
        dil = t.shape[1]
        return t.reshape(B, dil, pairs, rows, (top // dil) * LANES)

    def spec(t):
        dil = t.shape[1]
        return pl.BlockSpec((1, 1, pairs, rows, LANES), lambda b, r: (b, r % dil, 0, 0, r // dil))

    width = pairs * LANES
    out = pl.pallas_call(
        _group_mix_kernel,
        grid=(B, top),
        in_specs=[spec(t) for t in outs] + [spec(t) for t in lses],
        out_specs=pl.BlockSpec((1, rows, width), lambda b, r: (b, 0, r)),
        out_shape=jax.ShapeDtypeStruct((B, rows, top * width), BF16),
        compiler_params=_params("parallel", "parallel"),
        name="group_mix",
    )(*[view(t) for t in outs], *[view(t) for t in lses])
    return out.reshape(B, S, width)


def _mem_kv_kernel(m_ref, g_ref, w_ref, ones_ref, kg_ref, k_ref, v_ref):
    h = _rms_rows(m_ref[0], g_ref[...]).astype(BF16)
    kv = jnp.dot(h, w_ref[...], preferred_element_type=F32)
    half = kv.shape[1] // 2
    for p in range(half // LANES):
        y = _pair_norm(kv[:, p * LANES:(p + 1) * LANES], ones_ref[...], kg_ref[...])
        k_ref[0, :, p * LANES:(p + 1) * LANES] = y.astype(BF16)
    v_ref[0] = kv[:, half:].astype(BF16)


def _mem_kv(mem, g, w, ones, kg):
    B, M, D = mem.shape
    width = w.shape[1] // 2
    blk = pl.BlockSpec((1, M, width), lambda b: (b, 0, 0))
    return pl.pallas_call(
        _mem_kv_kernel,
        grid=(B,),
        in_specs=[pl.BlockSpec((1, M, D), lambda b: (b, 0, 0)), _full(g.shape), _full(w.shape),
                  _full(ones.shape), _full(kg.shape)],
        out_specs=[blk, blk],
        out_shape=[jax.ShapeDtypeStruct((B, M, width), BF16)] * 2,
        compiler_params=_params("parallel"),
        name="mem_kv",
    )(mem, g, w, ones, kg)


def _cross_attn_kernel(q_ref, k_ref, v_ref, o_ref):
    for p in range(q_ref.shape[1]):
        cols = slice(p * LANES, (p + 1) * LANES)
        k, v = k_ref[0, :, cols], v_ref[0, :, cols]
        outs = []
        for qh in _split_heads(q_ref[0, p]):
            o, l = _softmax_pv(qh, k, v)
            outs.append(o / l)
        o_ref[0, :, cols] = _merge_heads(*outs).astype(BF16)


def _cross_attn(xq, mk, mv):
    B, P, S, _ = xq.shape
    M, width = mk.shape[1:]
    ts = ROW_TILE
    kv = pl.BlockSpec((1, M, width), lambda b, i: (b, 0, 0))
    return pl.pallas_call(
        _cross_attn_kernel,
        grid=(B, S // ts),
        in_specs=[pl.BlockSpec((1, P, ts, LANES), lambda b, i: (b, 0, i, 0)), kv, kv],
        out_specs=pl.BlockSpec((1, ts, width), lambda b, i: (b, i, 0)),
        out_shape=jax.ShapeDtypeStruct((B, S, width), BF16),
        compiler_params=_params("parallel", "parallel"),
        name="cross_attn",
    )(xq, mk, mv)


def _outproj_kernel(mix_ref, cross_ref, x_ref, w_ref, o_ref):
    cm = mix_ref.shape[2]
    o_ref[0] = (x_ref[0]
                + jnp.dot(mix_ref[0], w_ref[:cm, :], preferred_element_type=F32)
                + jnp.dot(cross_ref[0], w_ref[cm:, :], preferred_element_type=F32))


def _outproj(mixed, cross, x, w):
    B, S, D = x.shape
    tm = ROW_TILE
    row = lambda c: pl.BlockSpec((1, tm, c), lambda b, i: (b, i, 0))
    return pl.pallas_call(
        _outproj_kernel,
        grid=(B, S // tm),
        in_specs=[row(mixed.shape[2]), row(cross.shape[2]), row(D), _full(w.shape)],
        out_specs=row(D),
        out_shape=jax.ShapeDtypeStruct((B, S, D), F32),
        compiler_params=_params("parallel", "parallel"),
        name="outproj",
    )(mixed, cross, x, w)


FF_CHUNK = 256


def _ffn_up_kernel(x_ref, g_ref, w_ref, u_ref, h_ref):
    h_ref[...] = _rms_rows(x_ref[0], g_ref[...]).astype(BF16)
    for c in range(w_ref.shape[1] // FF_CHUNK):
        cols = slice(c * FF_CHUNK, (c + 1) * FF_CHUNK)
        u_ref[0, :, cols] = jnp.dot(h_ref[...], w_ref[:, cols], preferred_element_type=F32).astype(BF16)


def _ffn_up(x, g, w):
    B, S, D = x.shape
    N = w.shape[1]
    tm = ROW_TILE
    return pl.pallas_call(
        _ffn_up_kernel,
        grid=(B, S // tm),
        in_specs=[pl.BlockSpec((1, tm, D), lambda b, i: (b, i, 0)), _full(g.shape), _full(w.shape)],
        out_specs=pl.BlockSpec((1, tm, N), lambda b, i: (b, i, 0)),
        out_shape=jax.ShapeDtypeStruct((B, S, N), BF16),
        scratch_shapes=[pltpu.VMEM((tm, D), BF16)],
        compiler_params=_params("parallel", "parallel"),
        name="ffn_up",
    )(x, g, w)


HALO_ROWS = 16


def _ffn_down_kernel(u_ref, prev_ref, next_ref, cw_ref, cb_ref, x_ref, w_ref, o_ref, g_ref):
    tm = u_ref.shape[1]
    i, n = pl.program_id(1), pl.num_programs(1)
    has_prev = (i > 0).astype(F32)
    has_next = (i < n - 1).astype(F32)
    row = lax.broadcasted_iota(jnp.int32, (tm, FF_CHUNK), 0)

    def conv(c0):
        cols = slice(c0, c0 + FF_CHUNK)
        u = u_ref[0, :, cols].astype(F32)
        before = prev_ref[0, :, cols].astype(F32)[HALO_ROWS - 1:HALO_ROWS] * has_prev
        after = next_ref[0, :, cols].astype(F32)[0:1] * has_next
        up = jnp.where(row == 0, before, pltpu.roll(u, 1, axis=0))
        dn = jnp.where(row == tm - 1, after, pltpu.roll(u, tm - 1, axis=0))
        return (up * cw_ref[0:1, cols] + u * cw_ref[1:2, cols] + dn * cw_ref[2:3, cols]
                + cb_ref[:, cols])

    for c0 in range(0, D_FF, FF_CHUNK):
        a, b = conv(c0), conv(D_FF + c0)
        g_ref[:, c0:c0 + FF_CHUNK] = (a / (1.0 + jnp.exp(-a)) * b).astype(BF16)
    o_ref[0] = x_ref[0] + jnp.dot(g_ref[...], w_ref[...], preferred_element_type=F32)


def _ffn_down(u, conv_w, conv_b, x, w):
    B, S, D = x.shape
    N = u.shape[2]
    tm = ROW_TILE
    halo_per_tile = tm // HALO_ROWS
    last_halo = S // HALO_ROWS - 1
    return pl.pallas_call(
        _ffn_down_kernel,
        grid=(B, S // tm),
        in_specs=[pl.BlockSpec((1, tm, N), lambda b, i: (b, i, 0)),
                  pl.BlockSpec((1, HALO_ROWS, N), lambda b, i: (b, jnp.maximum(i * halo_per_tile - 1, 0), 0)),
                  pl.BlockSpec((1, HALO_ROWS, N),
                               lambda b, i: (b, jnp.minimum((i + 1) * halo_per_tile, last_halo), 0)),
                  _full(conv_w.shape), _full(conv_b.shape),
                  pl.BlockSpec((1, tm, D), lambda b, i: (b, i, 0)), _full(w.shape)],
        out_specs=pl.BlockSpec((1, tm, D), lambda b, i: (b, i, 0)),
        out_shape=jax.ShapeDtypeStruct((B, S, D), F32),
        scratch_shapes=[pltpu.VMEM((tm, D_FF), BF16)],
        compiler_params=_params("parallel", "parallel"),
        name="ffn_down",
    )(u, u, u, conv_w, conv_b, x, w)


def _rope_tables(S):
    inv = ROPE_THETA ** (-(jnp.arange(ROT_HALF, dtype=F32) * 2.0 / (2 * ROT_HALF)))
    ang = jnp.arange(S, dtype=F32)[:, None] * inv[None, :]
    cos, sin = jnp.cos(ang), jnp.sin(ang)
    rest = HEAD_DIM - 2 * ROT_HALF
    one, zero = jnp.ones((S, rest), F32), jnp.zeros((S, rest), F32)
    zhalf = jnp.zeros((S, ROT_HALF), F32)
    per_head = (jnp.concatenate([cos, cos, one], axis=1),
                jnp.concatenate([zhalf, sin, zero], axis=1),
                jnp.concatenate([-sin, zhalf, zero], axis=1))
    return tuple(jnp.tile(t, (1, LANES // HEAD_DIM)) for t in per_head)


def _pair_gain(g, scale=1.0):
    g = jnp.atleast_2d(g.astype(F32) * scale)
    return jnp.tile(g, (1, LANES // HEAD_DIM))


def _group_ones():
    head = jnp.arange(LANES) // HEAD_DIM
    return (head[:, None] == head[None, :]).astype(BF16)


def _a_weight(w):
    d = w.shape[0]
    n = A_HEADS * HEAD_DIM
    pair = lambda t: t.reshape(d, 2, A_HEADS, HEAD_DIM).transpose(0, 2, 1, 3).reshape(d, 2 * n)
    return jnp.concatenate([pair(w[:, :2 * n]), pair(w[:, 2 * n:4 * n]), w[:, 4 * n:]], axis=1).astype(BF16)


def _trunk(x, mem, P):
    S = x.shape[1]
    rope = _rope_tables(S)
    ones = _group_ones()
    n_layers = P["norm_mix"].shape[0]
    for i in range(n_layers):
        j = i // 2
        row = lambda name: P[name][i][None, :].astype(F32)
        mk, mv = _mem_kv(mem, row("norm_mem"), P["w_mem_kv"][i].astype(BF16), ones,
                         _pair_gain(P["xk_norm"][i]))
        xg = _pair_gain(P["xq_norm"][i], Q_SCALE)
        if i % 2 == 0:
            qg = jnp.tile(_pair_gain(P["a_q_norm"][j], Q_SCALE), (A_HEADS, 1))
            kg = jnp.tile(_pair_gain(P["a_k_norm"][j]), (A_HEADS, 1))
            q, k, v, xq = _inproj(x, row("norm_mix"), _a_weight(P["a_w_in"][j]), ones, rope,
                                  qg, kg, xg, A_HEADS)
            lam_init = 0.8 - 0.6 * math.exp(-0.3 * i)
            subln = P["a_subln"][j][None, :].astype(F32) * (1.0 - lam_init)
            mixed = _diff_attn(q[:, 0], k[:, 0], v[:, 0], P["a_lambda"][j].astype(F32), subln, lam_init)
            w_out = P["a_w_out"][j]
        else:
            pairs = B_HEADS // 2
            width = B_HEADS * HEAD_DIM
            w_in = P["b_w_in"][j]
            outs, lses = [], []
            for gi, (window, dil) in enumerate(B_GROUPS):
                cols = [w_in[:, s * len(B_GROUPS) * width + gi * width:][:, :width] for s in range(3)]
                if gi == 0:
                    cols.append(w_in[:, 3 * len(B_GROUPS) * width:])
                qg = jnp.tile(_pair_gain(P["b_q_norm"][j][gi], Q_SCALE), (pairs, 1))
                kg = jnp.tile(_pair_gain(P["b_k_norm"][j][gi]), (pairs, 1))
                q, k, v, *rest = _inproj(x, row("norm_mix"), jnp.concatenate(cols, axis=1).astype(BF16),
                                         ones, rope, qg, kg, xg, pairs, dil=dil, with_xq=(gi == 0))
                if gi == 0:
                    xq = rest[0]
                o, l = _band_attn(q, k, v, (window // 2) // dil)
                outs.append(o)
                lses.append(l)
            mixed = _group_mix(outs, lses)
            w_out = P["b_w_out"][j]
        cross = _cross_attn(xq[:, 0], mk, mv)
        x = _outproj(mixed, cross, x, w_out.astype(BF16))
        u = _ffn_up(x, row("norm_ffn"), P["w_up"][i].astype(BF16))
        x = _ffn_down(u, P["conv_w"][i].astype(F32), P["conv_b"][i][None, :].astype(F32), x,
                      P["w_down"][i].astype(BF16))
    return x


def kernel(x_prompt, x_sample, mem_prompt, mem_sample, norm_mix, norm_mem, w_mem_kv, xq_norm, xk_norm, a_w_in, a_w_out, a_q_norm, a_k_norm, a_lambda, a_subln, b_w_in, b_w_out, b_q_norm, b_k_norm, norm_ffn, w_up, conv_w, conv_b, w_down):
    P = dict(norm_mix=norm_mix, norm_mem=norm_mem, w_mem_kv=w_mem_kv, xq_norm=xq_norm, xk_norm=xk_norm,
             a_w_in=a_w_in, a_w_out=a_w_out, a_q_norm=a_q_norm, a_k_norm=a_k_norm, a_lambda=a_lambda,
             a_subln=a_subln, b_w_in=b_w_in, b_w_out=b_w_out, b_q_norm=b_q_norm, b_k_norm=b_k_norm,
             norm_ffn=norm_ffn, w_up=w_up, conv_w=conv_w, conv_b=conv_b, w_down=w_down)
    return _trunk(x_prompt, mem_prompt, P), _trunk(x_sample, mem_sample, P)
```

```python
import functools
import math

import jax
import jax.numpy as jnp
from jax import lax
from jax.experimental import pallas as pl
from jax.experimental.pallas import tpu as pltpu

F32 = jnp.float32
BF16 = jnp.bfloat16

LANES = 128
HEAD_DIM = 64
ROT_HALF = 8
A_HEADS = 8
B_HEADS = 8
B_GROUPS = ((128, 1), (512, 4), (2048, 16))
X_HEADS = 4
D_FF = 2816
ROPE_THETA = 500000.0
EPS = 1e-6
NEG_INF = -1e30
LOG2E = math.log2(math.e)
Q_SCALE = LOG2E / math.sqrt(HEAD_DIM)

V7X_VMEM_BYTES = 64 * 1024 * 1024
VMEM_LIMIT = V7X_VMEM_BYTES * 7 // 8

ROW_TILE = 512
Q_TILE_A = 1024
Q_SUB_A = 256
KEY_CHUNK = 512
Q_TILE_B = 128
MXU_COLS = 256


def _params(*sem):
    return pltpu.CompilerParams(dimension_semantics=sem, vmem_limit_bytes=VMEM_LIMIT)


def _full(shape):
    return pl.BlockSpec(shape, lambda *_: (0,) * len(shape))


def _rms_rows(x, g):
    ms = jnp.mean(x * x, axis=-1, keepdims=True)
    return x * lax.rsqrt(ms + EPS) * g


def _scale_rows(y, ssq, gain):
    return y * lax.rsqrt(ssq * (1.0 / HEAD_DIM) + EPS) * gain


def _pair_norm(y, group_ones, gain):
    ssq = jnp.dot((y * y).astype(BF16), group_ones, preferred_element_type=F32)
    return _scale_rows(y, ssq, gain)


def _rope(y, cos, sin_lo, sin_hi):
    return (y * cos + pltpu.roll(y, ROT_HALF, axis=1) * sin_lo
            + pltpu.roll(y, LANES - ROT_HALF, axis=1) * sin_hi)


def _split_heads(q):
    lane = lax.broadcasted_iota(jnp.int32, q.shape, 1)
    zero = jnp.zeros_like(q)
    return jnp.where(lane < HEAD_DIM, q, zero), jnp.where(lane >= HEAD_DIM, q, zero)


def _merge_heads(o_lo, o_hi):
    lane = lax.broadcasted_iota(jnp.int32, o_lo.shape, 1)
    return jnp.where(lane < HEAD_DIM, o_lo, o_hi)


def _scores(q, k):
    return lax.dot_general(q, k, (((1,), (1,)), ((), ())), preferred_element_type=F32)


def _inproj_kernel(nq, nk, nv, nx, x_ref, g_ref, w_ref, ones_ref, cos_ref, slo_ref, shi_ref,
                   qg_ref, kg_ref, xg_ref, *out_and_scratch):
    q_ref, k_ref, v_ref = out_and_scratch[:3]
    dil, tl = q_ref.shape[1], q_ref.shape[3]
    if dil == 1:
        h_ref = out_and_scratch[-1]
        h_ref[...] = _rms_rows(x_ref[0], g_ref[...]).astype(BF16)
    else:
        stage_ref, h_ref = out_and_scratch[-2:]
        h = _rms_rows(x_ref[0], g_ref[...])
        for c in range(stage_ref.shape[0]):
            stage_ref[c] = h[:, c * LANES:(c + 1) * LANES]
        for r in range(dil):
            for c in range(stage_ref.shape[0]):
                h_ref[r * tl:(r + 1) * tl, c * LANES:(c + 1) * LANES] = (
                    stage_ref[c, pl.ds(r, tl, stride=dil), :].astype(BF16))

    def put(ref, p, y):
        for r in range(dil):
            ref[0, r, p] = y[r * tl:(r + 1) * tl].astype(BF16)

    ones = ones_ref[...]
    cos, slo, shi = cos_ref[...], slo_ref[...], shi_ref[...]
    pairs_per_chunk = MXU_COLS // LANES
    n_chunks = (nq + nk + nv + nx) // pairs_per_chunk

    def project(ch):
        return jnp.dot(h_ref[...], w_ref[:, ch * MXU_COLS:(ch + 1) * MXU_COLS],
                       preferred_element_type=F32)

    acc_next = project(0)
    for ch in range(n_chunks):
        acc = acc_next
        if ch + 1 < n_chunks:
            acc_next = project(ch + 1)
        ys = [acc[:, half * LANES:(half + 1) * LANES] for half in range(pairs_per_chunk)]
        first = ch * pairs_per_chunk
        if first < nq + nk or first >= nq + nk + nv:
            ssq = [jnp.dot((y * y).astype(BF16), ones, preferred_element_type=F32) for y in ys]
        for half, y in enumerate(ys):
            p = first + half
            if p < nq + nk:
                ref, gain = (q_ref, qg_ref) if p < nq else (k_ref, kg_ref)
                p = p if p < nq else p - nq
                put(ref, p, _rope(_scale_rows(y, ssq[half], gain[p:p + 1, :]), cos, slo, shi))
            elif p < nq + nk + nv:
                put(v_ref, p - nq - nk, y)
            else:
                put(out_and_scratch[3], p - nq - nk - nv, _scale_rows(y, ssq[half], xg_ref[...]))


def _inproj(x, g, w, ones, rope, qg, kg, xg, nv, dil=1, with_xq=True):
    B, S, D = x.shape
    nq, nk, nx = qg.shape[0], kg.shape[0], (X_HEADS // 2 if with_xq else 0)
    tm = ROW_TILE
    tl = tm // dil
    rope = [t.reshape(S // tm, tl, dil, LANES).transpose(0, 2, 1, 3).reshape(S, LANES) for t in rope]
    tab = pl.BlockSpec((tm, LANES), lambda b, i: (i, 0))
    sizes = [nq, nk, nv] + ([nx] if with_xq else [])
    scratch = [pltpu.VMEM((tm, D), BF16)]
    if dil > 1:
        scratch.insert(0, pltpu.VMEM((D // LANES, tm, LANES), F32))
    return pl.pallas_call(
        functools.partial(_inproj_kernel, nq, nk, nv, nx),
        grid=(B, S // tm),
        in_specs=[pl.BlockSpec((1, tm, D), lambda b, i: (b, i, 0)), _full(g.shape), _full(w.shape),
                  _full(ones.shape), tab, tab, tab, _full(qg.shape), _full(kg.shape), _full(xg.shape)],
        out_specs=[pl.BlockSpec((1, dil, n, tl, LANES), lambda b, i: (b, 0, 0, i, 0)) for n in sizes],
        out_shape=[jax.ShapeDtypeStruct((B, dil, n, S // dil, LANES), BF16) for n in sizes],
        scratch_shapes=scratch,
        compiler_params=_params("parallel", "parallel"),
        name="inproj",
    )(x, g, w, ones, *rope, qg, kg, xg)


def _exp2_rows(s):
    e = jnp.exp2(s - jnp.max(s, axis=-1, keepdims=True))
    return e.astype(BF16), jnp.sum(e, axis=-1, keepdims=True)


def _softmax_pv(q, k, v):
    e, l = _exp2_rows(_scores(q, k))
    return jnp.dot(e, v, preferred_element_type=F32), l


def _lane_tiles(x):
    return [x[:, t * LANES:(t + 1) * LANES] for t in range(x.shape[1] // LANES)]


def _diff_attn_kernel(lam_init, q_ref, k_ref, v_ref, lam_ref, sg_ref, o_ref, s_ref):
    seq = k_ref.shape[2]
    rows, kc = s_ref.shape[1], KEY_CHUNK
    lp = lam_ref[...]
    lam = (jnp.exp(jnp.sum(lp[0:1] * lp[1:2], axis=-1, keepdims=True))
           - jnp.exp(jnp.sum(lp[2:3] * lp[3:4], axis=-1, keepdims=True)) + lam_init)
    units = []
    for sb in range(q_ref.shape[2] // rows):
        units += list(_split_heads(q_ref[0, 0, sb * rows:(sb + 1) * rows, :]))
    results = []
    m_prev = None
    for u in range(len(units) + 1):
        m_wide = l_wide = acc = None
        for c0 in range(0, seq, kc):
            if u < len(units):
                s = _scores(units[u], k_ref[0, 0, c0:c0 + kc, :])
                s_ref[u % 2, :, c0:c0 + kc] = s
                m_wide = functools.reduce(jnp.maximum, _lane_tiles(s) + ([] if m_wide is None else [m_wide]))
            if u > 0:
                e = jnp.exp2(s_ref[(u - 1) % 2, :, c0:c0 + kc] - m_prev)
                l_wide = functools.reduce(jnp.add, _lane_tiles(e) + ([] if l_wide is None else [l_wide]))
                pv = jnp.dot(e.astype(BF16), v_ref[0, 0, c0:c0 + kc, :], preferred_element_type=F32)
                acc = pv if acc is None else acc + pv
        if u > 0:
            results.append(acc / jnp.sum(l_wide, axis=-1, keepdims=True))
        if u < len(units):
            m_prev = jnp.max(m_wide, axis=-1, keepdims=True)
    for sb in range(len(units) // 2):
        o = results[2 * sb] - lam * results[2 * sb + 1]
        o_ref[0, sb * rows:(sb + 1) * rows, :] = _rms_rows(o, sg_ref[...]).astype(BF16)


def _diff_attn(q, k, v, lam_p, subln, lam_init):
    B, H, S, _ = q.shape
    tq = Q_TILE_A
    kv = pl.BlockSpec((1, 1, S, LANES), lambda b, h, i: (b, h, 0, 0))
    return pl.pallas_call(
        functools.partial(_diff_attn_kernel, lam_init),
        grid=(B, H, S // tq),
        in_specs=[pl.BlockSpec((1, 1, tq, LANES), lambda b, h, i: (b, h, i, 0)), kv, kv,
                  _full(lam_p.shape), _full(subln.shape)],
        out_specs=pl.BlockSpec((1, tq, LANES), lambda b, h, i: (b, i, h)),
        out_shape=jax.ShapeDtypeStruct((B, S, H * LANES), BF16),
        scratch_shapes=[pltpu.VMEM((2, Q_SUB_A, S), F32)],
        compiler_params=_params("parallel", "parallel", "parallel"),
        name="diff_attn",
    )(q, k, v, lam_p, subln)


def _band_attn_kernel(radius, q_ref, k_ref, v_ref, o_ref, l_ref):
    _, res, pairs, rows, _ = q_ref.shape
    L = k_ref.shape[3]
    tq = min(Q_TILE_B, L)
    span = min(L, tq + 2 * radius)
    row0 = pl.program_id(2) * rows
    col = lax.broadcasted_iota(jnp.int32, (tq, span), 1)
    qrow = lax.broadcasted_iota(jnp.int32, (tq, span), 0)
    for r in range(res):
        for blk in range(rows // tq):
            q0 = row0 + blk * tq
            start = pl.multiple_of(jnp.clip(q0 - radius, 0, L - span), radius)
            valid = jnp.abs(col - qrow + (start - q0)) <= radius
            rs = slice(blk * tq, (blk + 1) * tq)
            scores = [[_scores(qh, k_ref[0, r, p, pl.ds(start, span), :])
                       for qh in _split_heads(q_ref[0, r, p, rs, :])] for p in range(pairs)]
            probs = []
            for p in range(pairs):
                per_head = []
                for s in scores[p]:
                    s = jnp.where(valid, s, NEG_INF)
                    m = jnp.max(s, axis=-1, keepdims=True)
                    e = jnp.exp2(s - m)
                    z = jnp.sum(e, axis=-1, keepdims=True)
                    per_head.append((e.astype(BF16), z, m + jnp.log2(z)))
                probs.append(per_head)
            for p in range(pairs):
                vw = v_ref[0, r, p, pl.ds(start, span), :]
                outs = [jnp.dot(e, vw, preferred_element_type=F32) / z for e, z, _ in probs[p]]
                lses = [jnp.broadcast_to(lse, (tq, LANES)) for _, _, lse in probs[p]]
                o_ref[0, r, p, rs, :] = _merge_heads(*outs).astype(BF16)
                l_ref[0, r, p, rs, :] = _merge_heads(*lses)


def _band_attn(q, k, v, radius):
    B, dil, pairs, L, _ = q.shape
    rows = min(L, ROW_TILE)
    res = ROW_TILE // rows
    kv = pl.BlockSpec((1, res, pairs, L, LANES), lambda b, r, i: (b, r, 0, 0, 0))
    tile = pl.BlockSpec((1, res, pairs, rows, LANES), lambda b, r, i: (b, r, 0, i, 0))
    return pl.pallas_call(
        functools.partial(_band_attn_kernel, radius),
        grid=(B, dil // res, L // rows),
        in_specs=[tile, kv, kv],
        out_specs=[tile, tile],
        out_shape=[jax.ShapeDtypeStruct(q.shape, BF16), jax.ShapeDtypeStruct(q.shape, F32)],
        compiler_params=_params("parallel", "parallel", "parallel"),
        name="band_attn",
    )(q, k, v)


def _group_mix_kernel(*refs):
    out_ref, stage_ref = refs[-2:]
    n = (len(refs) - 2) // 2
    o_refs, l_refs = refs[:n], refs[n:2 * n]

    def token_order(ref, p, slot):
        dil, tl = ref.shape[1], ref.shape[3]
        if dil == 1:
            return ref[0, 0, p].astype(F32)
        for r in range(dil):
            stage_ref[slot, pl.ds(r, tl, stride=dil), :] = ref[0, r, p].astype(F32)
        return stage_ref[slot]

    for p in range(o_refs[0].shape[2]):
        ls = [token_order(l, p, 2 * g) for g, l in enumerate(l_refs)]
        os_ = [token_order(o, p, 2 * g + 1) for g, o in enumerate(o_refs)]
        m = functools.reduce(jnp.maximum, ls)
        ws = [jnp.exp2(l - m) for l in ls]
        num = functools.reduce(jnp.add, [w * o for w, o in zip(ws, os_)])
        out_ref[0, :, p * LANES:(p + 1) * LANES] = (num / functools.reduce(jnp.add, ws)).astype(BF16)


def _group_mix(outs, lses):
    B, _, pairs, S, _ = outs[0].shape
    tm = ROW_TILE

    def spec(t):
        dil = t.shape[1]
        return pl.BlockSpec((1, dil, pairs, tm // dil, LANES), lambda b, i: (b, 0, 0, i, 0))

    width = pairs * LANES
    return pl.pallas_call(
        _group_mix_kernel,
        grid=(B, S // tm),
        in_specs=[spec(t) for t in outs] + [spec(t) for t in lses],
        out_specs=pl.BlockSpec((1, tm, width), lambda b, i: (b, i, 0)),
        out_shape=jax.ShapeDtypeStruct((B, S, width), BF16),
        scratch_shapes=[pltpu.VMEM((2 * len(outs), tm, LANES), F32)],
        compiler_params=_params("parallel", "parallel"),
        name="group_mix",
    )(*outs, *lses)


def _mem_kv_kernel(m_ref, g_ref, w_ref, ones_ref, kg_ref, k_ref, v_ref):
    h = _rms_rows(m_ref[0], g_ref[...]).astype(BF16)
    kv = jnp.dot(h, w_ref[...], preferred_element_type=F32)
    half = kv.shape[1] // 2
    for p in range(half // LANES):
        y = _pair_norm(kv[:, p * LANES:(p + 1) * LANES], ones_ref[...], kg_ref[...])
        k_ref[0, :, p * LANES:(p + 1) * LANES] = y.astype(BF16)
    v_ref[0] = kv[:, half:].astype(BF16)


def _mem_kv(mem, g, w, ones, kg):
    B, M, D = mem.shape
    width = w.shape[1] // 2
    blk = pl.BlockSpec((1, M, width), lambda b: (b, 0, 0))
    return pl.pallas_call(
        _mem_kv_kernel,
        grid=(B,),
        in_specs=[pl.BlockSpec((1, M, D), lambda b: (b, 0, 0)), _full(g.shape), _full(w.shape),
                  _full(ones.shape), _full(kg.shape)],
        out_specs=[blk, blk],
        out_shape=[jax.ShapeDtypeStruct((B, M, width), BF16)] * 2,
        compiler_params=_params("parallel"),
        name="mem_kv",
    )(mem, g, w, ones, kg)


def _cross_attn_kernel(q_ref, k_ref, v_ref, o_ref):
    for p in range(q_ref.shape[1]):
        cols = slice(p * LANES, (p + 1) * LANES)
        k, v = k_ref[0, :, cols], v_ref[0, :, cols]
        outs = []
        for qh in _split_heads(q_ref[0, p]):
            o, l = _softmax_pv(qh, k, v)
            outs.append(o / l)
        o_ref[0, :, cols] = _merge_heads(*outs).astype(BF16)


def _cross_attn(xq, mk, mv):
    B, P, S, _ = xq.shape
    M, width = mk.shape[1:]
    ts = ROW_TILE
    kv = pl.BlockSpec((1, M, width), lambda b, i: (b, 0, 0))
    return pl.pallas_call(
        _cross_attn_kernel,
        grid=(B, S // ts),
        in_specs=[pl.BlockSpec((1, P, ts, LANES), lambda b, i: (b, 0, i, 0)), kv, kv],
        out_specs=pl.BlockSpec((1, ts, width), lambda b, i: (b, i, 0)),
        out_shape=jax.ShapeDtypeStruct((B, S, width), BF16),
        compiler_params=_params("parallel", "parallel"),
        name="cross_attn",
    )(xq, mk, mv)


def _outproj_kernel(mix_ref, cross_ref, x_ref, w_ref, o_ref):
    cm = mix_ref.shape[2]
    o_ref[0] = (x_ref[0]
                + jnp.dot(mix_ref[0], w_ref[:cm, :], preferred_element_type=F32)
                + jnp.dot(cross_ref[0], w_ref[cm:, :], preferred_element_type=F32))


def _outproj(mixed, cross, x, w):
    B, S, D = x.shape
    tm = ROW_TILE
    row = lambda c: pl.BlockSpec((1, tm, c), lambda b, i: (b, i, 0))
    return pl.pallas_call(
        _outproj_kernel,
        grid=(B, S // tm),
        in_specs=[row(mixed.shape[2]), row(cross.shape[2]), row(D), _full(w.shape)],
        out_specs=row(D),
        out_shape=jax.ShapeDtypeStruct((B, S, D), F32),
        compiler_params=_params("parallel", "parallel"),
        name="outproj",
    )(mixed, cross, x, w)


FF_CHUNK = 256


HALO_ROWS = 16


def _ffn_kernel(x_ref, prev_ref, next_ref, g_ref, wu_ref, cw_ref, cb_ref, wd_ref, o_ref,
                h_ref, stage_ref, act_ref):
    tm = x_ref.shape[1]
    i, n = pl.program_id(1), pl.num_programs(1)
    g = g_ref[...]
    h_ref[0:HALO_ROWS] = (_rms_rows(prev_ref[0], g) * (i > 0).astype(F32)).astype(BF16)
    h_ref[HALO_ROWS:HALO_ROWS + tm] = _rms_rows(x_ref[0], g).astype(BF16)
    h_ref[HALO_ROWS + tm:] = (_rms_rows(next_ref[0], g) * (i < n - 1).astype(F32)).astype(BF16)
    tiles = FF_CHUNK // LANES

    def project(c0):
        return [jnp.dot(h_ref[...], wu_ref[:, col:col + FF_CHUNK], preferred_element_type=F32)
                for col in (c0, D_FF + c0)]

    def conv(u, col, slot):
        parts = []
        for t in range(tiles):
            stage_ref[slot, t] = u[:, t * LANES:(t + 1) * LANES]
            cols = slice(col + t * LANES, col + (t + 1) * LANES)
            taps = [stage_ref[slot, t, pl.ds(HALO_ROWS - 1 + k, tm), :] * cw_ref[k:k + 1, cols]
                    for k in range(3)]
            parts.append(taps[0] + taps[1] + taps[2] + cb_ref[:, cols])
        return jnp.concatenate(parts, axis=1)

    starts = list(range(0, D_FF, FF_CHUNK))
    u_next = project(starts[0])
    for idx, c0 in enumerate(starts):
        u_a, u_b = u_next
        if idx + 1 < len(starts):
            u_next = project(starts[idx + 1])
        slot = 2 * (idx % 2)
        a, b = conv(u_a, c0, slot), conv(u_b, D_FF + c0, slot + 1)
        act_ref[:, c0:c0 + FF_CHUNK] = (a / (1.0 + jnp.exp(-a)) * b).astype(BF16)
    o_ref[0] = x_ref[0] + jnp.dot(act_ref[...], wd_ref[...], preferred_element_type=F32)


def _ffn(x, g, w_up, conv_w, conv_b, w_down):
    B, S, D = x.shape
    tm = ROW_TILE
    halo_per_tile = tm // HALO_ROWS
    last_halo = S // HALO_ROWS - 1
    tile = pl.BlockSpec((1, tm, D), lambda b, i: (b, i, 0))
    return pl.pallas_call(
        _ffn_kernel,
        grid=(B, S // tm),
        in_specs=[tile,
                  pl.BlockSpec((1, HALO_ROWS, D), lambda b, i: (b, jnp.maximum(i * halo_per_tile - 1, 0), 0)),
                  pl.BlockSpec((1, HALO_ROWS, D),
                               lambda b, i: (b, jnp.minimum((i + 1) * halo_per_tile, last_halo), 0)),
                  _full(g.shape), _full(w_up.shape), _full(conv_w.shape), _full(conv_b.shape),
                  _full(w_down.shape)],
        out_specs=tile,
        out_shape=jax.ShapeDtypeStruct((B, S, D), F32),
        scratch_shapes=[pltpu.VMEM((tm + 2 * HALO_ROWS, D), BF16),
                        pltpu.VMEM((4, FF_CHUNK // LANES, tm + 2 * HALO_ROWS, LANES), F32),
                        pltpu.VMEM((tm, D_FF), BF16)],
        compiler_params=_params("parallel", "parallel"),
        name="ffn",
    )(x, x, x, g, w_up, conv_w, conv_b, w_down)


def _rope_tables(S):
    inv = ROPE_THETA ** (-(jnp.arange(ROT_HALF, dtype=F32) * 2.0 / (2 * ROT_HALF)))
    ang = jnp.arange(S, dtype=F32)[:, None] * inv[None, :]
    cos, sin = jnp.cos(ang), jnp.sin(ang)
    rest = HEAD_DIM - 2 * ROT_HALF
    one, zero = jnp.ones((S, rest), F32), jnp.zeros((S, rest), F32)
    zhalf = jnp.zeros((S, ROT_HALF), F32)
    per_head = (jnp.concatenate([cos, cos, one], axis=1),
                jnp.concatenate([zhalf, sin, zero], axis=1),
                jnp.concatenate([-sin, zhalf, zero], axis=1))
    return tuple(jnp.tile(t, (1, LANES // HEAD_DIM)) for t in per_head)


def _pair_gain(g, scale=1.0):
    g = jnp.atleast_2d(g.astype(F32) * scale)
    return jnp.tile(g, (1, LANES // HEAD_DIM))


def _group_ones():
    head = jnp.arange(LANES) // HEAD_DIM
    return (head[:, None] == head[None, :]).astype(BF16)


def _a_weight(w):
    d = w.shape[0]
    n = A_HEADS * HEAD_DIM
    pair = lambda t: t.reshape(d, 2, A_HEADS, HEAD_DIM).transpose(0, 2, 1, 3).reshape(d, 2 * n)
    return jnp.concatenate([pair(w[:, :2 * n]), pair(w[:, 2 * n:4 * n]), w[:, 4 * n:]], axis=1).astype(BF16)


def _trunk(x, mem, P):
    S = x.shape[1]
    rope = _rope_tables(S)
    ones = _group_ones()
    n_layers = P["norm_mix"].shape[0]
    for i in range(n_layers):
        j = i // 2
        row = lambda name: P[name][i][None, :].astype(F32)
        mk, mv = _mem_kv(mem, row("norm_mem"), P["w_mem_kv"][i].astype(BF16), ones,
                         _pair_gain(P["xk_norm"][i]))
        xg = _pair_gain(P["xq_norm"][i], Q_SCALE)
        if i % 2 == 0:
            qg = jnp.tile(_pair_gain(P["a_q_norm"][j], Q_SCALE), (A_HEADS, 1))
            kg = jnp.tile(_pair_gain(P["a_k_norm"][j]), (A_HEADS, 1))
            q, k, v, xq = _inproj(x, row("norm_mix"), _a_weight(P["a_w_in"][j]), ones, rope,
                                  qg, kg, xg, A_HEADS)
            lam_init = 0.8 - 0.6 * math.exp(-0.3 * i)
            subln = P["a_subln"][j][None, :].astype(F32) * (1.0 - lam_init)
            mixed = _diff_attn(q[:, 0], k[:, 0], v[:, 0], P["a_lambda"][j].astype(F32), subln, lam_init)
            w_out = P["a_w_out"][j]
        else:
            pairs = B_HEADS // 2
            width = B_HEADS * HEAD_DIM
            w_in = P["b_w_in"][j]
            outs, lses = [], []
            for gi, (window, dil) in enumerate(B_GROUPS):
                cols = [w_in[:, s * len(B_GROUPS) * width + gi * width:][:, :width] for s in range(3)]
                if gi == 0:
                    cols.append(w_in[:, 3 * len(B_GROUPS) * width:])
                qg = jnp.tile(_pair_gain(P["b_q_norm"][j][gi], Q_SCALE), (pairs, 1))
                kg = jnp.tile(_pair_gain(P["b_k_norm"][j][gi]), (pairs, 1))
                q, k, v, *rest = _inproj(x, row("norm_mix"), jnp.concatenate(cols, axis=1).astype(BF16),
                                         ones, rope, qg, kg, xg, pairs, dil=dil, with_xq=(gi == 0))
                if gi == 0:
                    xq = rest[0]
                o, l = _band_attn(q, k, v, (window // 2) // dil)
                outs.append(o)
                lses.append(l)
            mixed = _group_mix(outs, lses)
            w_out = P["b_w_out"][j]
        cross = _cross_attn(xq[:, 0], mk, mv)
        x = _outproj(mixed, cross, x, w_out.astype(BF16))
        x = _ffn(x, row("norm_ffn"), P["w_up"][i].astype(BF16), P["conv_w"][i].astype(F32),
                 P["conv_b"][i][None, :].astype(F32), P["w_down"][i].astype(BF16))
    return x


def kernel(x_prompt, x_sample, mem_prompt, mem_sample, norm_mix, norm_mem, w_mem_kv, xq_norm, xk_norm, a_w_in, a_w_out, a_q_norm, a_k_norm, a_lambda, a_subln, b_w_in, b_w_out, b_q_norm, b_k_norm, norm_ffn, w_up, conv_w, conv_b, w_down):
    P = dict(norm_mix=norm_mix, norm_mem=norm_mem, w_mem_kv=w_mem_kv, xq_norm=xq_norm, xk_norm=xk_norm,
             a_w_in=a_w_in, a_w_out=a_w_out, a_q_norm=a_q_norm, a_k_norm=a_k_norm, a_lambda=a_lambda,
             a_subln=a_subln, b_w_in=b_w_in, b_w_out=b_w_out, b_q_norm=b_q_norm, b_k_norm=b_k_norm,
             norm_ffn=norm_ffn, w_up=w_up, conv_w=conv_w, conv_b=conv_b, w_down=w_down)
    return _trunk(x_prompt, mem_prompt, P), _trunk(x_sample, mem_sample, P)
```

```python
import functools
import math

import jax
import jax.numpy as jnp
from jax import lax
from jax.experimental import pallas as pl
from jax.experimental.pallas import tpu as pltpu

F32 = jnp.float32
BF16 = jnp.bfloat16

LANES = 128
HEAD_DIM = 64
ROT_HALF = 8
A_HEADS = 8
B_HEADS = 8
B_GROUPS = ((128, 1), (512, 4), (2048, 16))
X_HEADS = 4
D_FF = 2816
ROPE_THETA = 500000.0
EPS = 1e-6
NEG_INF = -1e30
LOG2E = math.log2(math.e)
Q_SCALE = LOG2E / math.sqrt(HEAD_DIM)

V7X_VMEM_BYTES = 64 * 1024 * 1024
VMEM_LIMIT = V7X_VMEM_BYTES * 7 // 8

ROW_TILE = 512
INPROJ_ROW_TILE = 1024
FFN_ROW_TILE = 512
Q_TILE_A = 1024
Q_SUB_A = 256
KEY_CHUNK = 512
Q_TILE_B = 128
MXU_COLS = 256


def _params(*sem):
    return pltpu.CompilerParams(dimension_semantics=sem, vmem_limit_bytes=VMEM_LIMIT)


def _full(shape):
    return pl.BlockSpec(shape, lambda *_: (0,) * len(shape))


def _resident(shape):
    return pl.BlockSpec(shape, lambda *_: (0,) * len(shape), pipeline_mode=pl.Buffered(1))


def _rms_rows(x, g):
    ms = jnp.mean(x * x, axis=-1, keepdims=True)
    return x * lax.rsqrt(ms + EPS) * g


def _lane_tiles(x):
    return [x[:, t * LANES:(t + 1) * LANES] for t in range(x.shape[1] // LANES)]


def _scale_rows(y, ssq, gain):
    return y * lax.rsqrt(ssq * (1.0 / HEAD_DIM) + EPS) * gain


def _pair_norm(y, group_ones, gain):
    ssq = jnp.dot((y * y).astype(BF16), group_ones, preferred_element_type=F32)
    return _scale_rows(y, ssq, gain)


def _rope(y, cos, sin_lo, sin_hi):
    return (y * cos + pltpu.roll(y, ROT_HALF, axis=1) * sin_lo
            + pltpu.roll(y, LANES - ROT_HALF, axis=1) * sin_hi)


def _split_heads(q):
    lane = lax.broadcasted_iota(jnp.int32, q.shape, 1)
    zero = jnp.zeros_like(q)
    return jnp.where(lane < HEAD_DIM, q, zero), jnp.where(lane >= HEAD_DIM, q, zero)


def _merge_heads(o_lo, o_hi):
    lane = lax.broadcasted_iota(jnp.int32, o_lo.shape, 1)
    return jnp.where(lane < HEAD_DIM, o_lo, o_hi)


def _scores(q, k):
    return lax.dot_general(q, k, (((1,), (1,)), ((), ())), preferred_element_type=F32)


def _inproj_kernel(nq, nk, nv, nx, x_ref, g_ref, w_ref, ones_ref, cos_ref, slo_ref, shi_ref,
                   qg_ref, kg_ref, xg_ref, *out_and_scratch):
    q_ref, k_ref, v_ref = out_and_scratch[:3]
    dil, tl = q_ref.shape[1], q_ref.shape[3]
    if dil == 1:
        h_ref = out_and_scratch[-1]
        h_ref[...] = _rms_rows(x_ref[0], g_ref[...]).astype(BF16)
    else:
        stage_ref, h_ref = out_and_scratch[-2:]
        h = _rms_rows(x_ref[0], g_ref[...])
        for c in range(stage_ref.shape[0]):
            stage_ref[c] = h[:, c * LANES:(c + 1) * LANES]
        for r in range(dil):
            for c in range(stage_ref.shape[0]):
                h_ref[r * tl:(r + 1) * tl, c * LANES:(c + 1) * LANES] = (
                    stage_ref[c, pl.ds(r, tl, stride=dil), :].astype(BF16))

    def put(ref, p, y):
        for r in range(dil):
            ref[0, r, p] = y[r * tl:(r + 1) * tl].astype(BF16)

    ones = ones_ref[...]
    cos, slo, shi = cos_ref[...], slo_ref[...], shi_ref[...]
    pairs_per_chunk = MXU_COLS // LANES
    n_chunks = (nq + nk + nv + nx) // pairs_per_chunk

    def project(ch):
        return jnp.dot(h_ref[...], w_ref[:, ch * MXU_COLS:(ch + 1) * MXU_COLS],
                       preferred_element_type=F32)

    acc_next = project(0)
    for ch in range(n_chunks):
        acc = acc_next
        if ch + 1 < n_chunks:
            acc_next = project(ch + 1)
        ys = _lane_tiles(acc)
        first = ch * pairs_per_chunk
        if first < nq + nk or first >= nq + nk + nv:
            ssq = _lane_tiles(jnp.dot((acc * acc).astype(BF16), ones, preferred_element_type=F32))
        for half, y in enumerate(ys):
            p = first + half
            if p < nq + nk:
                ref, gain = (q_ref, qg_ref) if p < nq else (k_ref, kg_ref)
                p = p if p < nq else p - nq
                put(ref, p, _rope(_scale_rows(y, ssq[half], gain[p:p + 1, :]), cos, slo, shi))
            elif p < nq + nk + nv:
                put(v_ref, p - nq - nk, y)
            else:
                put(out_and_scratch[3], p - nq - nk - nv, _scale_rows(y, ssq[half], xg_ref[...]))


def _inproj(x, g, w, ones, rope, qg, kg, xg, nv, dil=1, with_xq=True):
    B, S, D = x.shape
    nq, nk, nx = qg.shape[0], kg.shape[0], (X_HEADS // 2 if with_xq else 0)
    tm = INPROJ_ROW_TILE
    tl = tm // dil
    rope = [t.reshape(S // tm, tl, dil, LANES).transpose(0, 2, 1, 3).reshape(S, LANES) for t in rope]
    tab = pl.BlockSpec((tm, LANES), lambda b, i: (i, 0))
    sizes = [nq, nk, nv] + ([nx] if with_xq else [])
    scratch = [pltpu.VMEM((tm, D), BF16)]
    if dil > 1:
        scratch.insert(0, pltpu.VMEM((D // LANES, tm, LANES), F32))
    return pl.pallas_call(
        functools.partial(_inproj_kernel, nq, nk, nv, nx),
        grid=(B, S // tm),
        in_specs=[pl.BlockSpec((1, tm, D), lambda b, i: (b, i, 0)), _full(g.shape), _full(w.shape),
                  _full(ones.shape), tab, tab, tab, _full(qg.shape), _full(kg.shape), _full(xg.shape)],
        out_specs=[pl.BlockSpec((1, dil, n, tl, LANES), lambda b, i: (b, 0, 0, i, 0)) for n in sizes],
        out_shape=[jax.ShapeDtypeStruct((B, dil, n, S // dil, LANES), BF16) for n in sizes],
        scratch_shapes=scratch,
        compiler_params=_params("parallel", "parallel"),
        name="inproj",
    )(x, g, w, ones, *rope, qg, kg, xg)


def _exp2_rows(s):
    e = jnp.exp2(s - jnp.max(s, axis=-1, keepdims=True))
    return e.astype(BF16), jnp.sum(e, axis=-1, keepdims=True)


def _diff_attn_kernel(lam_init, q_ref, k_ref, v_ref, lam_ref, sg_ref, o_ref, s_ref):
    seq = k_ref.shape[2]
    rows, kc = s_ref.shape[1], KEY_CHUNK
    lp = lam_ref[...]
    lam = (jnp.exp(jnp.sum(lp[0:1] * lp[1:2], axis=-1, keepdims=True))
           - jnp.exp(jnp.sum(lp[2:3] * lp[3:4], axis=-1, keepdims=True)) + lam_init)
    units = []
    for sb in range(q_ref.shape[2] // rows):
        units += list(_split_heads(q_ref[0, 0, sb * rows:(sb + 1) * rows, :]))
    ones_col = (lax.broadcasted_iota(jnp.int32, (kc, LANES), 1) == 0).astype(BF16)
    results = []
    m_prev = None
    for u in range(len(units) + 1):
        m_wide = acc = None
        for c0 in range(0, seq, kc):
            if u < len(units):
                s = _scores(units[u], k_ref[0, 0, c0:c0 + kc, :])
                s_ref[u % 2, :, c0:c0 + kc] = s
                m_wide = functools.reduce(jnp.maximum, _lane_tiles(s) + ([] if m_wide is None else [m_wide]))
            if u > 0:
                e = jnp.exp2(s_ref[(u - 1) % 2, :, c0:c0 + kc] - m_prev)
                v_one = jnp.concatenate([v_ref[0, 0, c0:c0 + kc, :], ones_col], axis=1)
                pv = jnp.dot(e.astype(BF16), v_one, preferred_element_type=F32)
                acc = pv if acc is None else acc + pv
        if u > 0:
            results.append(acc[:, :LANES] / acc[:, LANES:LANES + 1])
        if u < len(units):
            m_prev = jnp.max(m_wide, axis=-1, keepdims=True)
    for sb in range(len(units) // 2):
        o = results[2 * sb] - lam * results[2 * sb + 1]
        o_ref[0, sb * rows:(sb + 1) * rows, :] = _rms_rows(o, sg_ref[...]).astype(BF16)


def _diff_attn(q, k, v, lam_p, subln, lam_init):
    B, H, S, _ = q.shape
    tq = Q_TILE_A
    kv = pl.BlockSpec((1, 1, S, LANES), lambda b, h, i: (b, h, 0, 0))
    return pl.pallas_call(
        functools.partial(_diff_attn_kernel, lam_init),
        grid=(B, H, S // tq),
        in_specs=[pl.BlockSpec((1, 1, tq, LANES), lambda b, h, i: (b, h, i, 0)), kv, kv,
                  _full(lam_p.shape), _full(subln.shape)],
        out_specs=pl.BlockSpec((1, tq, LANES), lambda b, h, i: (b, i, h)),
        out_shape=jax.ShapeDtypeStruct((B, S, H * LANES), BF16),
        scratch_shapes=[pltpu.VMEM((2, Q_SUB_A, S), F32)],
        compiler_params=_params("parallel", "parallel", "parallel"),
        name="diff_attn",
    )(q, k, v, lam_p, subln)


def _band_attn_kernel(radius, q_ref, k_ref, v_ref, o_ref, l_ref):
    _, res, pairs, rows, _ = q_ref.shape
    L = k_ref.shape[3]
    tq = min(Q_TILE_B, L)
    span = min(L, tq + 2 * radius)
    row0 = pl.program_id(2) * rows
    col = lax.broadcasted_iota(jnp.int32, (tq, span), 1)
    qrow = lax.broadcasted_iota(jnp.int32, (tq, span), 0)
    for r in range(res):
        for blk in range(rows // tq):
            q0 = row0 + blk * tq
            start = pl.multiple_of(jnp.clip(q0 - radius, 0, L - span), radius)
            valid = jnp.abs(col - qrow + (start - q0)) <= radius
            rs = slice(blk * tq, (blk + 1) * tq)
            scores = [[_scores(qh, k_ref[0, r, p, pl.ds(start, span), :])
                       for qh in _split_heads(q_ref[0, r, p, rs, :])] for p in range(pairs)]
            probs = []
            for p in range(pairs):
                per_head = []
                for s in scores[p]:
                    s = jnp.where(valid, s, NEG_INF)
                    m = jnp.max(s, axis=-1, keepdims=True)
                    e = jnp.exp2(s - m)
                    z = jnp.sum(e, axis=-1, keepdims=True)
                    per_head.append((e.astype(BF16), z, m + jnp.log2(z)))
                probs.append(per_head)
            for p in range(pairs):
                vw = v_ref[0, r, p, pl.ds(start, span), :]
                outs = [jnp.dot(e, vw, preferred_element_type=F32) / z for e, z, _ in probs[p]]
                lses = [jnp.broadcast_to(lse, (tq, LANES)) for _, _, lse in probs[p]]
                o_ref[0, r, p, rs, :] = _merge_heads(*outs).astype(BF16)
                l_ref[0, r, p, rs, :] = _merge_heads(*lses)


def _band_attn(q, k, v, radius):
    B, dil, pairs, L, _ = q.shape
    rows = min(L, ROW_TILE)
    res = ROW_TILE // rows
    kv = pl.BlockSpec((1, res, pairs, L, LANES), lambda b, r, i: (b, r, 0, 0, 0))
    tile = pl.BlockSpec((1, res, pairs, rows, LANES), lambda b, r, i: (b, r, 0, i, 0))
    return pl.pallas_call(
        functools.partial(_band_attn_kernel, radius),
        grid=(B, dil // res, L // rows),
        in_specs=[tile, kv, kv],
        out_specs=[tile, tile],
        out_shape=[jax.ShapeDtypeStruct(q.shape, BF16), jax.ShapeDtypeStruct(q.shape, F32)],
        compiler_params=_params("parallel", "parallel", "parallel"),
        name="band_attn",
    )(q, k, v)


def _group_mix_kernel(*refs):
    out_ref, stage_ref = refs[-2:]
    n = (len(refs) - 2) // 2
    o_refs, l_refs = refs[:n], refs[n:2 * n]

    def token_order(ref, p, slot):
        dil, tl = ref.shape[1], ref.shape[3]
        if dil == 1:
            return ref[0, 0, p].astype(F32)
        for r in range(dil):
            stage_ref[slot, pl.ds(r, tl, stride=dil), :] = ref[0, r, p].astype(F32)
        return stage_ref[slot]

    for p in range(o_refs[0].shape[2]):
        ls = [token_order(l, p, 2 * g) for g, l in enumerate(l_refs)]
        os_ = [token_order(o, p, 2 * g + 1) for g, o in enumerate(o_refs)]
        m = functools.reduce(jnp.maximum, ls)
        ws = [jnp.exp2(l - m) for l in ls]
        num = functools.reduce(jnp.add, [w * o for w, o in zip(ws, os_)])
        out_ref[0, :, p * LANES:(p + 1) * LANES] = (num / functools.reduce(jnp.add, ws)).astype(BF16)


def _group_mix(outs, lses):
    B, _, pairs, S, _ = outs[0].shape
    tm = ROW_TILE

    def spec(t):
        dil = t.shape[1]
        return pl.BlockSpec((1, dil, pairs, tm // dil, LANES), lambda b, i: (b, 0, 0, i, 0))

    width = pairs * LANES
    return pl.pallas_call(
        _group_mix_kernel,
        grid=(B, S // tm),
        in_specs=[spec(t) for t in outs] + [spec(t) for t in lses],
        out_specs=pl.BlockSpec((1, tm, width), lambda b, i: (b, i, 0)),
        out_shape=jax.ShapeDtypeStruct((B, S, width), BF16),
        scratch_shapes=[pltpu.VMEM((2 * len(outs), tm, LANES), F32)],
        compiler_params=_params("parallel", "parallel"),
        name="group_mix",
    )(*outs, *lses)


def _mem_kv_kernel(m_ref, g_ref, w_ref, ones_ref, kg_ref, k_ref, v_ref):
    h = _rms_rows(m_ref[0], g_ref[...]).astype(BF16)
    kv = jnp.dot(h, w_ref[...], preferred_element_type=F32)
    half = kv.shape[1] // 2
    for p in range(half // LANES):
        y = _pair_norm(kv[:, p * LANES:(p + 1) * LANES], ones_ref[...], kg_ref[...])
        k_ref[0, :, p * LANES:(p + 1) * LANES] = y.astype(BF16)
    v_ref[0] = kv[:, half:].astype(BF16)


def _mem_kv(mem, g, w, ones, kg):
    B, M, D = mem.shape
    width = w.shape[1] // 2
    blk = pl.BlockSpec((1, M, width), lambda b: (b, 0, 0))
    return pl.pallas_call(
        _mem_kv_kernel,
        grid=(B,),
        in_specs=[pl.BlockSpec((1, M, D), lambda b: (b, 0, 0)), _full(g.shape), _full(w.shape),
                  _full(ones.shape), _full(kg.shape)],
        out_specs=[blk, blk],
        out_shape=[jax.ShapeDtypeStruct((B, M, width), BF16)] * 2,
        compiler_params=_params("parallel"),
        name="mem_kv",
    )(mem, g, w, ones, kg)


def _cross_attn_kernel(q_ref, k_ref, v_ref, o_ref):
    pairs = range(q_ref.shape[1])
    cols = [slice(p * LANES, (p + 1) * LANES) for p in pairs]
    scores = [[_scores(qh, k_ref[0, :, cols[p]]) for qh in _split_heads(q_ref[0, p])] for p in pairs]
    probs = [[_exp2_rows(s) for s in scores[p]] for p in pairs]
    for p in pairs:
        v = v_ref[0, :, cols[p]]
        outs = [jnp.dot(e, v, preferred_element_type=F32) / l for e, l in probs[p]]
        o_ref[0, :, cols[p]] = _merge_heads(*outs).astype(BF16)


def _cross_attn(xq, mk, mv):
    B, P, S, _ = xq.shape
    M, width = mk.shape[1:]
    ts = ROW_TILE
    kv = pl.BlockSpec((1, M, width), lambda b, i: (b, 0, 0))
    return pl.pallas_call(
        _cross_attn_kernel,
        grid=(B, S // ts),
        in_specs=[pl.BlockSpec((1, P, ts, LANES), lambda b, i: (b, 0, i, 0)), kv, kv],
        out_specs=pl.BlockSpec((1, ts, width), lambda b, i: (b, i, 0)),
        out_shape=jax.ShapeDtypeStruct((B, S, width), BF16),
        compiler_params=_params("parallel", "parallel"),
        name="cross_attn",
    )(xq, mk, mv)


def _outproj_kernel(mix_ref, cross_ref, x_ref, w_ref, o_ref):
    cm = mix_ref.shape[2]
    o_ref[0] = (x_ref[0]
                + jnp.dot(mix_ref[0], w_ref[:cm, :], preferred_element_type=F32)
                + jnp.dot(cross_ref[0], w_ref[cm:, :], preferred_element_type=F32))


def _outproj(mixed, cross, x, w):
    B, S, D = x.shape
    tm = ROW_TILE
    row = lambda c: pl.BlockSpec((1, tm, c), lambda b, i: (b, i, 0))
    return pl.pallas_call(
        _outproj_kernel,
        grid=(B, S // tm),
        in_specs=[row(mixed.shape[2]), row(cross.shape[2]), row(D), _full(w.shape)],
        out_specs=row(D),
        out_shape=jax.ShapeDtypeStruct((B, S, D), F32),
        compiler_params=_params("parallel", "parallel"),
        name="outproj",
    )(mixed, cross, x, w)


FF_CHUNK = 256


HALO_ROWS = 16


def _ffn_kernel(x_ref, prev_ref, next_ref, g_ref, wu_ref, cw_ref, cb_ref, wd_ref, o_ref,
                h_ref, stage_ref, act_ref):
    tm = x_ref.shape[1]
    i, n = pl.program_id(1), pl.num_programs(1)
    g = g_ref[...]
    h_ref[0:HALO_ROWS] = (_rms_rows(prev_ref[0], g) * (i > 0).astype(F32)).astype(BF16)
    h_ref[HALO_ROWS:HALO_ROWS + tm] = _rms_rows(x_ref[0], g).astype(BF16)
    h_ref[HALO_ROWS + tm:] = (_rms_rows(next_ref[0], g) * (i < n - 1).astype(F32)).astype(BF16)
    tiles = FF_CHUNK // LANES

    def project(c0):
        return [jnp.dot(h_ref[...], wu_ref[:, col:col + FF_CHUNK], preferred_element_type=F32)
                for col in (c0, D_FF + c0)]

    def conv(u, col, slot):
        parts = []
        for t in range(tiles):
            stage_ref[slot, t] = u[:, t * LANES:(t + 1) * LANES]
            cols = slice(col + t * LANES, col + (t + 1) * LANES)
            taps = [stage_ref[slot, t, pl.ds(HALO_ROWS - 1 + k, tm), :] * cw_ref[k:k + 1, cols]
                    for k in range(3)]
            parts.append(taps[0] + taps[1] + taps[2] + cb_ref[:, cols])
        return jnp.concatenate(parts, axis=1)

    starts = list(range(0, D_FF, FF_CHUNK))
    u_next = project(starts[0])
    for idx, c0 in enumerate(starts):
        u_a, u_b = u_next
        if idx + 1 < len(starts):
            u_next = project(starts[idx + 1])
        slot = 2 * (idx % 2)
        a, b = conv(u_a, c0, slot), conv(u_b, D_FF + c0, slot + 1)
        act_ref[:, c0:c0 + FF_CHUNK] = (a / (1.0 + jnp.exp(-a)) * b).astype(BF16)
    o_ref[0] = x_ref[0] + jnp.dot(act_ref[...], wd_ref[...], preferred_element_type=F32)


def _ffn(x, g, w_up, conv_w, conv_b, w_down):
    B, S, D = x.shape
    tm = FFN_ROW_TILE
    halo_per_tile = tm // HALO_ROWS
    last_halo = S // HALO_ROWS - 1
    tile = pl.BlockSpec((1, tm, D), lambda b, i: (b, i, 0))
    return pl.pallas_call(
        _ffn_kernel,
        grid=(B, S // tm),
        in_specs=[tile,
                  pl.BlockSpec((1, HALO_ROWS, D), lambda b, i: (b, jnp.maximum(i * halo_per_tile - 1, 0), 0)),
                  pl.BlockSpec((1, HALO_ROWS, D),
                               lambda b, i: (b, jnp.minimum((i + 1) * halo_per_tile, last_halo), 0)),
                  _full(g.shape), _resident(w_up.shape), _full(conv_w.shape), _full(conv_b.shape),
                  _resident(w_down.shape)],
        out_specs=tile,
        out_shape=jax.ShapeDtypeStruct((B, S, D), F32),
        scratch_shapes=[pltpu.VMEM((tm + 2 * HALO_ROWS, D), BF16),
                        pltpu.VMEM((4, FF_CHUNK // LANES, tm + 2 * HALO_ROWS, LANES), F32),
                        pltpu.VMEM((tm, D_FF), BF16)],
        compiler_params=_params("parallel", "parallel"),
        name="ffn",
    )(x, x, x, g, w_up, conv_w, conv_b, w_down)


def _rope_tables(S):
    inv = ROPE_THETA ** (-(jnp.arange(ROT_HALF, dtype=F32) * 2.0 / (2 * ROT_HALF)))
    ang = jnp.arange(S, dtype=F32)[:, None] * inv[None, :]
    cos, sin = jnp.cos(ang), jnp.sin(ang)
    rest = HEAD_DIM - 2 * ROT_HALF
    one, zero = jnp.ones((S, rest), F32), jnp.zeros((S, rest), F32)
    zhalf = jnp.zeros((S, ROT_HALF), F32)
    per_head = (jnp.concatenate([cos, cos, one], axis=1),
                jnp.concatenate([zhalf, sin, zero], axis=1),
                jnp.concatenate([-sin, zhalf, zero], axis=1))
    return tuple(jnp.tile(t, (1, LANES // HEAD_DIM)) for t in per_head)


def _pair_gain(g, scale=1.0):
    g = jnp.atleast_2d(g.astype(F32) * scale)
    return jnp.tile(g, (1, LANES // HEAD_DIM))


def _group_ones(width):
    head = jnp.arange(width) // HEAD_DIM
    return (head[:, None] == head[None, :]).astype(BF16)


def _a_weight(w):
    d = w.shape[0]
    n = A_HEADS * HEAD_DIM
    pair = lambda t: t.reshape(d, 2, A_HEADS, HEAD_DIM).transpose(0, 2, 1, 3).reshape(d, 2 * n)
    return jnp.concatenate([pair(w[:, :2 * n]), pair(w[:, 2 * n:4 * n]), w[:, 4 * n:]], axis=1).astype(BF16)


def _trunk(x, mem, P):
    S = x.shape[1]
    rope = _rope_tables(S)
    ones = _group_ones(MXU_COLS)
    n_layers = P["norm_mix"].shape[0]
    for i in range(n_layers):
        j = i // 2
        row = lambda name: P[name][i][None, :].astype(F32)
        mk, mv = _mem_kv(mem, row("norm_mem"), P["w_mem_kv"][i].astype(BF16), _group_ones(LANES),
                         _pair_gain(P["xk_norm"][i]))
        xg = _pair_gain(P["xq_norm"][i], Q_SCALE)
        if i % 2 == 0:
            qg = jnp.tile(_pair_gain(P["a_q_norm"][j], Q_SCALE), (A_HEADS, 1))
            kg = jnp.tile(_pair_gain(P["a_k_norm"][j]), (A_HEADS, 1))
            q, k, v, xq = _inproj(x, row("norm_mix"), _a_weight(P["a_w_in"][j]), ones, rope,
                                  qg, kg, xg, A_HEADS)
            lam_init = 0.8 - 0.6 * math.exp(-0.3 * i)
            subln = P["a_subln"][j][None, :].astype(F32) * (1.0 - lam_init)
            mixed = _diff_attn(q[:, 0], k[:, 0], v[:, 0], P["a_lambda"][j].astype(F32), subln, lam_init)
            w_out = P["a_w_out"][j]
        else:
            pairs = B_HEADS // 2
            width = B_HEADS * HEAD_DIM
            w_in = P["b_w_in"][j]
            outs, lses = [], []
            for gi, (window, dil) in enumerate(B_GROUPS):
                cols = [w_in[:, s * len(B_GROUPS) * width + gi * width:][:, :width] for s in range(3)]
                if gi == 0:
                    cols.append(w_in[:, 3 * len(B_GROUPS) * width:])
                qg = jnp.tile(_pair_gain(P["b_q_norm"][j][gi], Q_SCALE), (pairs, 1))
                kg = jnp.tile(_pair_gain(P["b_k_norm"][j][gi]), (pairs, 1))
                q, k, v, *rest = _inproj(x, row("norm_mix"), jnp.concatenate(cols, axis=1).astype(BF16),
                                         ones, rope, qg, kg, xg, pairs, dil=dil, with_xq=(gi == 0))
                if gi == 0:
                    xq = rest[0]
                o, l = _band_attn(q, k, v, (window // 2) // dil)
                outs.append(o)
                lses.append(l)
            mixed = _group_mix(outs, lses)
            w_out = P["b_w_out"][j]
        cross = _cross_attn(xq[:, 0], mk, mv)
        x = _outproj(mixed, cross, x, w_out.astype(BF16))
        x = _ffn(x, row("norm_ffn"), P["w_up"][i].astype(BF16), P["conv_w"][i].astype(F32),
                 P["conv_b"][i][None, :].astype(F32), P["w_down"][i].astype(BF16))
    return x


def kernel(x_prompt, x_sample, mem_prompt, mem_sample, norm_mix, norm_mem, w_mem_kv, xq_norm, xk_norm, a_w_in, a_w_out, a_q_norm, a_k_norm, a_lambda, a_subln, b_w_in, b_w_out, b_q_norm, b_k_norm, norm_ffn, w_up, conv_w, conv_b, w_down):
    P = dict(norm_mix=norm_mix, norm_mem=norm_mem, w_mem_kv=w_mem_kv, xq_norm=xq_norm, xk_norm=xk_norm,
             a_w_in=a_w_in, a_w_out=a_w_out, a_q_norm=a_q_norm, a_k_norm=a_k_norm, a_lambda=a_lambda,
             a_subln=a_subln, b_w_in=b_w_in, b_w_out=b_w_out, b_q_norm=b_q_norm, b_k_norm=b_k_norm,
             norm_ffn=norm_ffn, w_up=w_up, conv_w=conv_w, conv_b=conv_b, w_down=w_down)
    return _trunk(x_prompt, mem_prompt, P), _trunk(x_sample, mem_sample, P)
```

```python
import functools
import math

import jax
import jax.numpy as jnp
from jax import lax
from jax.experimental import pallas as pl
from jax.experimental.pallas import tpu as pltpu

F32 = jnp.float32
BF16 = jnp.bfloat16

LANES = 128
HEAD_DIM = 64
ROT_HALF = 8
A_HEADS = 8
B_HEADS = 8
B_GROUPS = ((128, 1), (512, 4), (2048, 16))
X_HEADS = 4
D_FF = 2816
ROPE_THETA = 500000.0
EPS = 1e-6
NEG_INF = -1e30
LOG2E = math.log2(math.e)
Q_SCALE = LOG2E / math.sqrt(HEAD_DIM)

V7X_VMEM_BYTES = 64 * 1024 * 1024
VMEM_LIMIT = V7X_VMEM_BYTES * 7 // 8

ROW_TILE = 512
INPROJ_ROW_TILE = 1024
FFN_ROW_TILE = 512
Q_TILE_A = 1024
Q_SUB_A = 256
KEY_CHUNK = 512
SAFE_EXP2_RANGE = 50.0
NORM_SLACK = 1.02
Q_TILE_B = 128
MXU_COLS = 256


def _params(*sem):
    return pltpu.CompilerParams(dimension_semantics=sem, vmem_limit_bytes=VMEM_LIMIT)


def _full(shape):
    return pl.BlockSpec(shape, lambda *_: (0,) * len(shape))


def _resident(shape):
    return pl.BlockSpec(shape, lambda *_: (0,) * len(shape), pipeline_mode=pl.Buffered(1))


def _rms_rows(x, g):
    ms = jnp.mean(x * x, axis=-1, keepdims=True)
    return x * lax.rsqrt(ms + EPS) * g


def _lane_tiles(x):
    return [x[:, t * LANES:(t + 1) * LANES] for t in range(x.shape[1] // LANES)]


def _scale_rows(y, ssq, gain):
    return y * lax.rsqrt(ssq * (1.0 / HEAD_DIM) + EPS) * gain


def _pair_norm(y, group_ones, gain):
    ssq = jnp.dot((y * y).astype(BF16), group_ones, preferred_element_type=F32)
    return _scale_rows(y, ssq, gain)


def _rope(y, cos, sin_lo, sin_hi):
    return (y * cos + pltpu.roll(y, ROT_HALF, axis=1) * sin_lo
            + pltpu.roll(y, LANES - ROT_HALF, axis=1) * sin_hi)


def _split_heads(q):
    lane = lax.broadcasted_iota(jnp.int32, q.shape, 1)
    zero = jnp.zeros_like(q)
    return jnp.where(lane < HEAD_DIM, q, zero), jnp.where(lane >= HEAD_DIM, q, zero)


def _merge_heads(o_lo, o_hi):
    lane = lax.broadcasted_iota(jnp.int32, o_lo.shape, 1)
    return jnp.where(lane < HEAD_DIM, o_lo, o_hi)


def _scores(q, k):
    return lax.dot_general(q, k, (((1,), (1,)), ((), ())), preferred_element_type=F32)


def _inproj_kernel(nq, nk, nv, nx, x_ref, g_ref, w_ref, ones_ref, cos_ref, slo_ref, shi_ref,
                   qg_ref, kg_ref, xg_ref, *out_and_scratch):
    q_ref, k_ref, v_ref = out_and_scratch[:3]
    dil, tl = q_ref.shape[1], q_ref.shape[3]
    if dil == 1:
        h_ref = out_and_scratch[-1]
        h_ref[...] = _rms_rows(x_ref[0], g_ref[...]).astype(BF16)
    else:
        stage_ref, h_ref = out_and_scratch[-2:]
        h = _rms_rows(x_ref[0], g_ref[...])
        for c in range(stage_ref.shape[0]):
            stage_ref[c] = h[:, c * LANES:(c + 1) * LANES]
        for r in range(dil):
            for c in range(stage_ref.shape[0]):
                h_ref[r * tl:(r + 1) * tl, c * LANES:(c + 1) * LANES] = (
                    stage_ref[c, pl.ds(r, tl, stride=dil), :].astype(BF16))

    def put(ref, p, y):
        for r in range(dil):
            ref[0, r, p] = y[r * tl:(r + 1) * tl].astype(BF16)

    ones = ones_ref[...]
    cos, slo, shi = cos_ref[...], slo_ref[...], shi_ref[...]
    pairs_per_chunk = MXU_COLS // LANES
    n_chunks = (nq + nk + nv + nx) // pairs_per_chunk

    def project(ch):
        return jnp.dot(h_ref[...], w_ref[:, ch * MXU_COLS:(ch + 1) * MXU_COLS],
                       preferred_element_type=F32)

    acc_next = project(0)
    for ch in range(n_chunks):
        acc = acc_next
        if ch + 1 < n_chunks:
            acc_next = project(ch + 1)
        ys = _lane_tiles(acc)
        first = ch * pairs_per_chunk
        if first < nq + nk or first >= nq + nk + nv:
            ssq = _lane_tiles(jnp.dot((acc * acc).astype(BF16), ones, preferred_element_type=F32))
        for half, y in enumerate(ys):
            p = first + half
            if p < nq + nk:
                ref, gain = (q_ref, qg_ref) if p < nq else (k_ref, kg_ref)
                p = p if p < nq else p - nq
                put(ref, p, _rope(_scale_rows(y, ssq[half], gain[p:p + 1, :]), cos, slo, shi))
            elif p < nq + nk + nv:
                put(v_ref, p - nq - nk, y)
            else:
                put(out_and_scratch[3], p - nq - nk - nv, _scale_rows(y, ssq[half], xg_ref[...]))


def _inproj(x, g, w, ones, rope, qg, kg, xg, nv, dil=1, with_xq=True):
    B, S, D = x.shape
    nq, nk, nx = qg.shape[0], kg.shape[0], (X_HEADS // 2 if with_xq else 0)
    tm = INPROJ_ROW_TILE
    tl = tm // dil
    rope = [t.reshape(S // tm, tl, dil, LANES).transpose(0, 2, 1, 3).reshape(S, LANES) for t in rope]
    tab = pl.BlockSpec((tm, LANES), lambda b, i: (i, 0))
    sizes = [nq, nk, nv] + ([nx] if with_xq else [])
    scratch = [pltpu.VMEM((tm, D), BF16)]
    if dil > 1:
        scratch.insert(0, pltpu.VMEM((D // LANES, tm, LANES), F32))
    return pl.pallas_call(
        functools.partial(_inproj_kernel, nq, nk, nv, nx),
        grid=(B, S // tm),
        in_specs=[pl.BlockSpec((1, tm, D), lambda b, i: (b, i, 0)), _full(g.shape), _full(w.shape),
                  _full(ones.shape), tab, tab, tab, _full(qg.shape), _full(kg.shape), _full(xg.shape)],
        out_specs=[pl.BlockSpec((1, dil, n, tl, LANES), lambda b, i: (b, 0, 0, i, 0)) for n in sizes],
        out_shape=[jax.ShapeDtypeStruct((B, dil, n, S // dil, LANES), BF16) for n in sizes],
        scratch_shapes=scratch,
        compiler_params=_params("parallel", "parallel"),
        name="inproj",
    )(x, g, w, ones, *rope, qg, kg, xg)


def _exp2_rows(s):
    e = jnp.exp2(s - jnp.max(s, axis=-1, keepdims=True))
    return e.astype(BF16), jnp.sum(e, axis=-1, keepdims=True)


def _diff_attn_kernel(lam_init, shifted, q_ref, k_ref, v_ref, lam_ref, sg_ref, o_ref, s_ref):
    seq = k_ref.shape[2]
    ring, _, rows, _ = s_ref.shape
    kc = KEY_CHUNK
    n_sub = q_ref.shape[2] // rows
    lp = lam_ref[...]
    lam = (jnp.exp(jnp.sum(lp[0:1] * lp[1:2], axis=-1, keepdims=True))
           - jnp.exp(jnp.sum(lp[2:3] * lp[3:4], axis=-1, keepdims=True)) + lam_init)
    wide = lambda col: jnp.broadcast_to(col, (rows, LANES))
    lag = 1 if shifted else 0
    row_max, coef = {}, {}
    for it in range(n_sub + lag + 1):
        sb_qk, sb_exp, sb_pv = it, it - lag, it - lag - 1
        do_qk, do_exp, do_pv = sb_qk < n_sub, 0 <= sb_exp < n_sub, 0 <= sb_pv
        if do_qk:
            qs = _split_heads(q_ref[0, 0, sb_qk * rows:(sb_qk + 1) * rows, :])
            m_wide = [None, None]
        if do_exp:
            l_wide = [None, None]
        acc = None
        for c0 in range(0, seq, kc):
            cols = slice(c0, c0 + kc)
            if do_qk:
                for h in range(2):
                    s = _scores(qs[h], k_ref[0, 0, cols, :])
                    if shifted:
                        s_ref[sb_qk % ring, h, :, cols] = s
                        m_wide[h] = functools.reduce(
                            jnp.maximum, _lane_tiles(s) + ([] if m_wide[h] is None else [m_wide[h]]))
                    else:
                        e = [jnp.exp2(t) for t in _lane_tiles(s)]
                        s_ref[sb_qk % ring, h, :, cols] = jnp.concatenate(e, axis=1)
                        l_wide[h] = functools.reduce(jnp.add, e + ([] if l_wide[h] is None else [l_wide[h]]))
            if shifted and do_exp:
                for h in range(2):
                    e = [jnp.exp2(t - row_max[sb_exp][h]) for t in _lane_tiles(s_ref[sb_exp % ring, h, :, cols])]
                    s_ref[sb_exp % ring, h, :, cols] = jnp.concatenate(e, axis=1)
                    l_wide[h] = functools.reduce(jnp.add, e + ([] if l_wide[h] is None else [l_wide[h]]))
            if do_pv:
                c1, c2 = coef[sb_pv]
                diff = [t1 * c1 - t2 * c2 for t1, t2 in zip(_lane_tiles(s_ref[sb_pv % ring, 0, :, cols]),
                                                           _lane_tiles(s_ref[sb_pv % ring, 1, :, cols]))]
                pv = jnp.dot(jnp.concatenate(diff, axis=1).astype(BF16), v_ref[0, 0, cols, :],
                             preferred_element_type=F32)
                acc = pv if acc is None else acc + pv
        if shifted and do_qk:
            row_max[sb_qk] = [wide(jnp.max(m, axis=-1, keepdims=True)) for m in m_wide]
        if do_exp:
            l1, l2 = [jnp.sum(l, axis=-1, keepdims=True) for l in l_wide]
            coef[sb_exp] = (wide(1.0 / l1), wide(lam / l2))
        if do_pv:
            o_ref[0, sb_pv * rows:(sb_pv + 1) * rows, :] = _rms_rows(acc, sg_ref[...]).astype(BF16)


def _diff_attn_call(shifted, lam_init, q, k, v, lam_p, subln):
    B, H, S, _ = q.shape
    tq = Q_TILE_A
    kv = pl.BlockSpec((1, 1, S, LANES), lambda b, h, i: (b, h, 0, 0))
    stages_in_flight = 3 if shifted else 2
    return pl.pallas_call(
        functools.partial(_diff_attn_kernel, lam_init, shifted),
        grid=(B, H, S // tq),
        in_specs=[pl.BlockSpec((1, 1, tq, LANES), lambda b, h, i: (b, h, i, 0)), kv, kv,
                  _full(lam_p.shape), _full(subln.shape)],
        out_specs=pl.BlockSpec((1, tq, LANES), lambda b, h, i: (b, i, h)),
        out_shape=jax.ShapeDtypeStruct((B, S, H * LANES), BF16),
        scratch_shapes=[pltpu.VMEM((stages_in_flight, 2, Q_SUB_A, S), F32)],
        compiler_params=_params("parallel", "parallel", "parallel"),
        name="diff_attn_shifted" if shifted else "diff_attn",
    )(q, k, v, lam_p, subln)


def _diff_attn(q, k, v, lam_p, subln, lam_init, score_bound):
    return lax.cond(score_bound <= SAFE_EXP2_RANGE,
                    functools.partial(_diff_attn_call, False, lam_init),
                    functools.partial(_diff_attn_call, True, lam_init),
                    q, k, v, lam_p, subln)


def _band_attn_kernel(radius, q_ref, k_ref, v_ref, o_ref, l_ref):
    _, res, pairs, rows, _ = q_ref.shape
    L = k_ref.shape[3]
    tq = min(Q_TILE_B, L)
    span = min(L, tq + 2 * radius)
    row0 = pl.program_id(2) * rows
    col = lax.broadcasted_iota(jnp.int32, (tq, span), 1)
    qrow = lax.broadcasted_iota(jnp.int32, (tq, span), 0)
    for r in range(res):
        for blk in range(rows // tq):
            q0 = row0 + blk * tq
            start = pl.multiple_of(jnp.clip(q0 - radius, 0, L - span), radius)
            valid = jnp.abs(col - qrow + (start - q0)) <= radius
            rs = slice(blk * tq, (blk + 1) * tq)
            scores = [[_scores(qh, k_ref[0, r, p, pl.ds(start, span), :])
                       for qh in _split_heads(q_ref[0, r, p, rs, :])] for p in range(pairs)]
            probs = []
            for p in range(pairs):
                per_head = []
                for s in scores[p]:
                    s = jnp.where(valid, s, NEG_INF)
                    m = jnp.max(s, axis=-1, keepdims=True)
                    e = jnp.exp2(s - m)
                    z = jnp.sum(e, axis=-1, keepdims=True)
                    per_head.append((e.astype(BF16), z, m + jnp.log2(z)))
                probs.append(per_head)
            for p in range(pairs):
                vw = v_ref[0, r, p, pl.ds(start, span), :]
                outs = [jnp.dot(e, vw, preferred_element_type=F32) / z for e, z, _ in probs[p]]
                lses = [jnp.broadcast_to(lse, (tq, LANES)) for _, _, lse in probs[p]]
                o_ref[0, r, p, rs, :] = _merge_heads(*outs).astype(BF16)
                l_ref[0, r, p, rs, :] = _merge_heads(*lses)


def _band_attn(q, k, v, radius):
    B, dil, pairs, L, _ = q.shape
    rows = min(L, ROW_TILE)
    res = ROW_TILE // rows
    kv = pl.BlockSpec((1, res, pairs, L, LANES), lambda b, r, i: (b, r, 0, 0, 0))
    tile = pl.BlockSpec((1, res, pairs, rows, LANES), lambda b, r, i: (b, r, 0, i, 0))
    return pl.pallas_call(
        functools.partial(_band_attn_kernel, radius),
        grid=(B, dil // res, L // rows),
        in_specs=[tile, kv, kv],
        out_specs=[tile, tile],
        out_shape=[jax.ShapeDtypeStruct(q.shape, BF16), jax.ShapeDtypeStruct(q.shape, F32)],
        compiler_params=_params("parallel", "parallel", "parallel"),
        name="band_attn",
    )(q, k, v)


def _group_mix_into(o_refs, l_refs, stage_ref, dst_ref):
    def token_order(ref, p, slot):
        dil, tl = ref.shape[1], ref.shape[3]
        if dil == 1:
            return ref[0, 0, p].astype(F32)
        for r in range(dil):
            stage_ref[slot, pl.ds(r, tl, stride=dil), :] = ref[0, r, p].astype(F32)
        return stage_ref[slot]

    for p in range(o_refs[0].shape[2]):
        ls = [token_order(l, p, 2 * g) for g, l in enumerate(l_refs)]
        os_ = [token_order(o, p, 2 * g + 1) for g, o in enumerate(o_refs)]
        m = functools.reduce(jnp.maximum, ls)
        ws = [jnp.exp2(l - m) for l in ls]
        num = functools.reduce(jnp.add, [w * o for w, o in zip(ws, os_)])
        dst_ref[:, p * LANES:(p + 1) * LANES] = (num / functools.reduce(jnp.add, ws)).astype(BF16)


def _mem_kv_kernel(m_ref, g_ref, w_ref, ones_ref, kg_ref, k_ref, v_ref):
    h = _rms_rows(m_ref[0], g_ref[...]).astype(BF16)
    kv = jnp.dot(h, w_ref[...], preferred_element_type=F32)
    half = kv.shape[1] // 2
    for p in range(half // LANES):
        y = _pair_norm(kv[:, p * LANES:(p + 1) * LANES], ones_ref[...], kg_ref[...])
        k_ref[0, :, p * LANES:(p + 1) * LANES] = y.astype(BF16)
    v_ref[0] = kv[:, half:].astype(BF16)


def _mem_kv(mem, g, w, ones, kg):
    B, M, D = mem.shape
    width = w.shape[1] // 2
    blk = pl.BlockSpec((1, M, width), lambda b: (b, 0, 0))
    return pl.pallas_call(
        _mem_kv_kernel,
        grid=(B,),
        in_specs=[pl.BlockSpec((1, M, D), lambda b: (b, 0, 0)), _full(g.shape), _full(w.shape),
                  _full(ones.shape), _full(kg.shape)],
        out_specs=[blk, blk],
        out_shape=[jax.ShapeDtypeStruct((B, M, width), BF16)] * 2,
        compiler_params=_params("parallel"),
        name="mem_kv",
    )(mem, g, w, ones, kg)


def _cross_attn_into(q_ref, k_ref, v_ref, dst_ref, col0):
    pairs = range(q_ref.shape[2])
    cols = [slice(p * LANES, (p + 1) * LANES) for p in pairs]
    scores = [[_scores(qh, k_ref[0, :, cols[p]]) for qh in _split_heads(q_ref[0, 0, p])] for p in pairs]
    probs = [[_exp2_rows(s) for s in scores[p]] for p in pairs]
    for p in pairs:
        v = v_ref[0, :, cols[p]]
        outs = [jnp.dot(e, v, preferred_element_type=F32) / l for e, l in probs[p]]
        dst_ref[:, col0 + p * LANES:col0 + (p + 1) * LANES] = _merge_heads(*outs).astype(BF16)


def _outproj_kernel(n_groups, *refs):
    if n_groups:
        o_refs, l_refs = refs[:n_groups], refs[n_groups:2 * n_groups]
        xq_ref, mk_ref, mv_ref, x_ref, w_ref, out_ref, lhs_ref, stage_ref = refs[2 * n_groups:]
        mix_width = o_refs[0].shape[2] * LANES
        _group_mix_into(o_refs, l_refs, stage_ref, lhs_ref)
        _cross_attn_into(xq_ref, mk_ref, mv_ref, lhs_ref, mix_width)
        out_ref[0] = x_ref[0] + jnp.dot(lhs_ref[...], w_ref[...], preferred_element_type=F32)
    else:
        mix_ref, xq_ref, mk_ref, mv_ref, x_ref, w_ref, out_ref, lhs_ref = refs
        mix_width = mix_ref.shape[2]
        _cross_attn_into(xq_ref, mk_ref, mv_ref, lhs_ref, 0)
        out_ref[0] = (x_ref[0]
                      + jnp.dot(mix_ref[0], w_ref[:mix_width, :], preferred_element_type=F32)
                      + jnp.dot(lhs_ref[...], w_ref[mix_width:, :], preferred_element_type=F32))


def _outproj(mixed, xq, mk, mv, x, w):
    B, S, D = x.shape
    M, cross_width = mk.shape[1:]
    tm = ROW_TILE
    row = lambda c: pl.BlockSpec((1, tm, c), lambda b, i: (b, i, 0))
    kv = pl.BlockSpec((1, M, cross_width), lambda b, i: (b, 0, 0))
    common = [pl.BlockSpec((1, 1, xq.shape[2], tm, LANES), lambda b, i: (b, 0, 0, i, 0)), kv, kv,
              row(D), _full(w.shape)]
    if isinstance(mixed, tuple):
        groups = [*mixed[0], *mixed[1]]
        pairs = groups[0].shape[2]
        group_spec = lambda t: pl.BlockSpec((1, t.shape[1], pairs, tm // t.shape[1], LANES),
                                            lambda b, i: (b, 0, 0, i, 0))
        in_specs = [group_spec(t) for t in groups] + common
        args = groups
        scratch = [pltpu.VMEM((tm, pairs * LANES + cross_width), BF16),
                   pltpu.VMEM((len(groups), tm, LANES), F32)]
        n_groups = len(mixed[0])
    else:
        in_specs = [row(mixed.shape[2])] + common
        args = [mixed]
        scratch = [pltpu.VMEM((tm, cross_width), BF16)]
        n_groups = 0
    return pl.pallas_call(
        functools.partial(_outproj_kernel, n_groups),
        grid=(B, S // tm),
        in_specs=in_specs,
        out_specs=row(D),
        out_shape=jax.ShapeDtypeStruct((B, S, D), F32),
        scratch_shapes=scratch,
        compiler_params=_params("parallel", "parallel"),
        name="outproj",
    )(*args, xq, mk, mv, x, w)


FF_CHUNK = 256


HALO_ROWS = 16


def _ffn_kernel(x_ref, prev_ref, next_ref, g_ref, wu_ref, cw_ref, cb_ref, wd_ref, o_ref,
                h_ref, stage_ref, act_ref):
    tm = x_ref.shape[1]
    i, n = pl.program_id(1), pl.num_programs(1)
    g = g_ref[...]
    h_ref[0:HALO_ROWS] = (_rms_rows(prev_ref[0], g) * (i > 0).astype(F32)).astype(BF16)
    h_ref[HALO_ROWS:HALO_ROWS + tm] = _rms_rows(x_ref[0], g).astype(BF16)
    h_ref[HALO_ROWS + tm:] = (_rms_rows(next_ref[0], g) * (i < n - 1).astype(F32)).astype(BF16)
    tiles = FF_CHUNK // LANES

    def project(c0):
        return [jnp.dot(h_ref[...], wu_ref[:, col:col + FF_CHUNK], preferred_element_type=F32)
                for col in (c0, D_FF + c0)]

    def conv(u, col, slot):
        parts = []
        for t in range(tiles):
            stage_ref[slot, t] = u[:, t * LANES:(t + 1) * LANES]
            cols = slice(col + t * LANES, col + (t + 1) * LANES)
            taps = [stage_ref[slot, t, pl.ds(HALO_ROWS - 1 + k, tm), :] * cw_ref[k:k + 1, cols]
                    for k in range(3)]
            parts.append(taps[0] + taps[1] + taps[2] + cb_ref[:, cols])
        return jnp.concatenate(parts, axis=1)

    starts = list(range(0, D_FF, FF_CHUNK))
    u_next = project(starts[0])
    for idx, c0 in enumerate(starts):
        u_a, u_b = u_next
        if idx + 1 < len(starts):
            u_next = project(starts[idx + 1])
        slot = 2 * (idx % 2)
        a, b = conv(u_a, c0, slot), conv(u_b, D_FF + c0, slot + 1)
        act_ref[:, c0:c0 + FF_CHUNK] = (a / (1.0 + jnp.exp(-a)) * b).astype(BF16)
    o_ref[0] = x_ref[0] + jnp.dot(act_ref[...], wd_ref[...], preferred_element_type=F32)


def _ffn(x, g, w_up, conv_w, conv_b, w_down):
    B, S, D = x.shape
    tm = FFN_ROW_TILE
    halo_per_tile = tm // HALO_ROWS
    last_halo = S // HALO_ROWS - 1
    tile = pl.BlockSpec((1, tm, D), lambda b, i: (b, i, 0))
    return pl.pallas_call(
        _ffn_kernel,
        grid=(B, S // tm),
        in_specs=[tile,
                  pl.BlockSpec((1, HALO_ROWS, D), lambda b, i: (b, jnp.maximum(i * halo_per_tile - 1, 0), 0)),
                  pl.BlockSpec((1, HALO_ROWS, D),
                               lambda b, i: (b, jnp.minimum((i + 1) * halo_per_tile, last_halo), 0)),
                  _full(g.shape), _resident(w_up.shape), _full(conv_w.shape), _full(conv_b.shape),
                  _resident(w_down.shape)],
        out_specs=tile,
        out_shape=jax.ShapeDtypeStruct((B, S, D), F32),
        scratch_shapes=[pltpu.VMEM((tm + 2 * HALO_ROWS, D), BF16),
                        pltpu.VMEM((4, FF_CHUNK // LANES, tm + 2 * HALO_ROWS, LANES), F32),
                        pltpu.VMEM((tm, D_FF), BF16)],
        compiler_params=_params("parallel", "parallel"),
        name="ffn",
    )(x, x, x, g, w_up, conv_w, conv_b, w_down)


def _rope_tables(S):
    inv = ROPE_THETA ** (-(jnp.arange(ROT_HALF, dtype=F32) * 2.0 / (2 * ROT_HALF)))
    ang = jnp.arange(S, dtype=F32)[:, None] * inv[None, :]
    cos, sin = jnp.cos(ang), jnp.sin(ang)
    rest = HEAD_DIM - 2 * ROT_HALF
    one, zero = jnp.ones((S, rest), F32), jnp.zeros((S, rest), F32)
    zhalf = jnp.zeros((S, ROT_HALF), F32)
    per_head = (jnp.concatenate([cos, cos, one], axis=1),
                jnp.concatenate([zhalf, sin, zero], axis=1),
                jnp.concatenate([-sin, zhalf, zero], axis=1))
    return tuple(jnp.tile(t, (1, LANES // HEAD_DIM)) for t in per_head)


def _pair_gain(g, scale=1.0):
    g = jnp.atleast_2d(g.astype(F32) * scale)
    return jnp.tile(g, (1, LANES // HEAD_DIM))


def _group_ones(width):
    head = jnp.arange(width) // HEAD_DIM
    return (head[:, None] == head[None, :]).astype(BF16)


def _a_weight(w):
    d = w.shape[0]
    n = A_HEADS * HEAD_DIM
    pair = lambda t: t.reshape(d, 2, A_HEADS, HEAD_DIM).transpose(0, 2, 1, 3).reshape(d, 2 * n)
    return jnp.concatenate([pair(w[:, :2 * n]), pair(w[:, 2 * n:4 * n]), w[:, 4 * n:]], axis=1).astype(BF16)


def _trunk(x, mem, P):
    S = x.shape[1]
    rope = _rope_tables(S)
    ones = _group_ones(MXU_COLS)
    n_layers = P["norm_mix"].shape[0]
    for i in range(n_layers):
        j = i // 2
        row = lambda name: P[name][i][None, :].astype(F32)
        mk, mv = _mem_kv(mem, row("norm_mem"), P["w_mem_kv"][i].astype(BF16), _group_ones(LANES),
                         _pair_gain(P["xk_norm"][i]))
        xg = _pair_gain(P["xq_norm"][i], Q_SCALE)
        if i % 2 == 0:
            qg = jnp.tile(_pair_gain(P["a_q_norm"][j], Q_SCALE), (A_HEADS, 1))
            kg = jnp.tile(_pair_gain(P["a_k_norm"][j]), (A_HEADS, 1))
            q, k, v, xq = _inproj(x, row("norm_mix"), _a_weight(P["a_w_in"][j]), ones, rope,
                                  qg, kg, xg, A_HEADS)
            lam_init = 0.8 - 0.6 * math.exp(-0.3 * i)
            subln = P["a_subln"][j][None, :].astype(F32) * (1.0 - lam_init)
            score_bound = (NORM_SLACK * HEAD_DIM * Q_SCALE * jnp.max(jnp.abs(P["a_q_norm"][j]))
                           * jnp.max(jnp.abs(P["a_k_norm"][j])))
            mixed = _diff_attn(q[:, 0], k[:, 0], v[:, 0], P["a_lambda"][j].astype(F32), subln, lam_init,
                               score_bound)
            w_out = P["a_w_out"][j]
        else:
            pairs = B_HEADS // 2
            width = B_HEADS * HEAD_DIM
            w_in = P["b_w_in"][j]
            outs, lses = [], []
            for gi, (window, dil) in enumerate(B_GROUPS):
                cols = [w_in[:, s * len(B_GROUPS) * width + gi * width:][:, :width] for s in range(3)]
                if gi == 0:
                    cols.append(w_in[:, 3 * len(B_GROUPS) * width:])
                qg = jnp.tile(_pair_gain(P["b_q_norm"][j][gi], Q_SCALE), (pairs, 1))
                kg = jnp.tile(_pair_gain(P["b_k_norm"][j][gi]), (pairs, 1))
                q, k, v, *rest = _inproj(x, row("norm_mix"), jnp.concatenate(cols, axis=1).astype(BF16),
                                         ones, rope, qg, kg, xg, pairs, dil=dil, with_xq=(gi == 0))
                if gi == 0:
                    xq = rest[0]
                o, l = _band_attn(q, k, v, (window // 2) // dil)
                outs.append(o)
                lses.append(l)
            mixed = (outs, lses)
            w_out = P["b_w_out"][j]
        x = _outproj(mixed, xq, mk, mv, x, w_out.astype(BF16))
        x = _ffn(x, row("norm_ffn"), P["w_up"][i].astype(BF16), P["conv_w"][i].astype(F32),
                 P["conv_b"][i][None, :].astype(F32), P["w_down"][i].astype(BF16))
    return x


def kernel(x_prompt, x_sample, mem_prompt, mem_sample, norm_mix, norm_mem, w_mem_kv, xq_norm, xk_norm, a_w_in, a_w_out, a_q_norm, a_k_norm, a_lambda, a_subln, b_w_in, b_w_out, b_q_norm, b_k_norm, norm_ffn, w_up, conv_w, conv_b, w_down):
    P = dict(norm_mix=norm_mix, norm_mem=norm_mem, w_mem_kv=w_mem_kv, xq_norm=xq_norm, xk_norm=xk_norm,
             a_w_in=a_w_in, a_w_out=a_w_out, a_q_norm=a_q_norm, a_k_norm=a_k_norm, a_lambda=a_lambda,
             a_subln=a_subln, b_w_in=b_w_in, b_w_out=b_w_out, b_q_norm=b_q_norm, b_k_norm=b_k_norm,
             norm_ffn=norm_ffn, w_up=w_up, conv_w=conv_w, conv_b=conv_b, w_down=w_down)
    return _trunk(x_prompt, mem_prompt, P), _trunk(x_sample, mem_sample, P)
```

```python
import functools
import math

import jax
import jax.numpy as jnp
from jax import lax
from jax.experimental import pallas as pl
from jax.experimental.pallas import tpu as pltpu

F32 = jnp.float32
BF16 = jnp.bfloat16

LANES = 128
HEAD_DIM = 64
ROT_HALF = 8
A_HEADS = 8
B_HEADS = 8
B_GROUPS = ((128, 1), (512, 4), (2048, 16))
X_HEADS = 4
D_FF = 2816
ROPE_THETA = 500000.0
EPS = 1e-6
NEG_INF = -1e30
LOG2E = math.log2(math.e)
Q_SCALE = LOG2E / math.sqrt(HEAD_DIM)

V7X_VMEM_BYTES = 64 * 1024 * 1024
VMEM_LIMIT = V7X_VMEM_BYTES * 7 // 8

ROW_TILE = 512
INPROJ_ROW_TILE = 1024
FFN_ROW_TILE = 512
Q_TILE_A = 2048
Q_SUB_A = 256
KEY_CHUNK = 512
SAFE_EXP2_RANGE = 50.0
NORM_SLACK = 1.02
Q_TILE_B = 128
MXU_COLS = 256


def _params(*sem):
    return pltpu.CompilerParams(dimension_semantics=sem, vmem_limit_bytes=VMEM_LIMIT)


def _full(shape):
    return pl.BlockSpec(shape, lambda *_: (0,) * len(shape))


def _resident(shape):
    return pl.BlockSpec(shape, lambda *_: (0,) * len(shape), pipeline_mode=pl.Buffered(1))


def _rms_rows(x, g):
    ms = jnp.mean(x * x, axis=-1, keepdims=True)
    return x * lax.rsqrt(ms + EPS) * g


def _lane_tiles(x):
    return [x[:, t * LANES:(t + 1) * LANES] for t in range(x.shape[1] // LANES)]


def _scale_rows(y, ssq, gain):
    return y * lax.rsqrt(ssq * (1.0 / HEAD_DIM) + EPS) * gain


def _pair_norm(y, group_ones, gain):
    ssq = jnp.dot((y * y).astype(BF16), group_ones, preferred_element_type=F32)
    return _scale_rows(y, ssq, gain)


def _rope(y, cos, sin_lo, sin_hi):
    return (y * cos + pltpu.roll(y, ROT_HALF, axis=1) * sin_lo
            + pltpu.roll(y, LANES - ROT_HALF, axis=1) * sin_hi)


def _split_heads(q):
    lane = lax.broadcasted_iota(jnp.int32, q.shape, 1)
    zero = jnp.zeros_like(q)
    return jnp.where(lane < HEAD_DIM, q, zero), jnp.where(lane >= HEAD_DIM, q, zero)


def _merge_heads(o_lo, o_hi):
    lane = lax.broadcasted_iota(jnp.int32, o_lo.shape, 1)
    return jnp.where(lane < HEAD_DIM, o_lo, o_hi)


def _scores(q, k):
    return lax.dot_general(q, k, (((1,), (1,)), ((), ())), preferred_element_type=F32)


def _inproj_kernel(nq, nk, nv, nx, x_ref, g_ref, w_ref, ones_ref, cos_ref, slo_ref, shi_ref,
                   qg_ref, kg_ref, xg_ref, *out_and_scratch):
    q_ref, k_ref, v_ref = out_and_scratch[:3]
    dil, tl = q_ref.shape[1], q_ref.shape[3]
    if dil == 1:
        h_ref = out_and_scratch[-1]
        h_ref[...] = _rms_rows(x_ref[0], g_ref[...]).astype(BF16)
    else:
        stage_ref, h_ref = out_and_scratch[-2:]
        h = _rms_rows(x_ref[0], g_ref[...])
        for c in range(stage_ref.shape[0]):
            stage_ref[c] = h[:, c * LANES:(c + 1) * LANES]
        for r in range(dil):
            for c in range(stage_ref.shape[0]):
                h_ref[r * tl:(r + 1) * tl, c * LANES:(c + 1) * LANES] = (
                    stage_ref[c, pl.ds(r, tl, stride=dil), :].astype(BF16))

    def put(ref, p, y):
        for r in range(dil):
            ref[0, r, p] = y[r * tl:(r + 1) * tl].astype(BF16)

    ones = ones_ref[...]
    cos, slo, shi = cos_ref[...], slo_ref[...], shi_ref[...]
    pairs_per_chunk = MXU_COLS // LANES
    n_chunks = (nq + nk + nv + nx) // pairs_per_chunk

    def project(ch):
        return jnp.dot(h_ref[...], w_ref[:, ch * MXU_COLS:(ch + 1) * MXU_COLS],
                       preferred_element_type=F32)

    acc_next = project(0)
    for ch in range(n_chunks):
        acc = acc_next
        if ch + 1 < n_chunks:
            acc_next = project(ch + 1)
        ys = _lane_tiles(acc)
        first = ch * pairs_per_chunk
        if first < nq + nk or first >= nq + nk + nv:
            ssq = _lane_tiles(jnp.dot((acc * acc).astype(BF16), ones, preferred_element_type=F32))
        for half, y in enumerate(ys):
            p = first + half
            if p < nq + nk:
                ref, gain = (q_ref, qg_ref) if p < nq else (k_ref, kg_ref)
                p = p if p < nq else p - nq
                put(ref, p, _rope(_scale_rows(y, ssq[half], gain[p:p + 1, :]), cos, slo, shi))
            elif p < nq + nk + nv:
                put(v_ref, p - nq - nk, y)
            else:
                put(out_and_scratch[3], p - nq - nk - nv, _scale_rows(y, ssq[half], xg_ref[...]))


def _inproj(x, g, w, ones, rope, qg, kg, xg, nv, dil=1, with_xq=True):
    B, S, D = x.shape
    nq, nk, nx = qg.shape[0], kg.shape[0], (X_HEADS // 2 if with_xq else 0)
    tm = INPROJ_ROW_TILE
    tl = tm // dil
    rope = [t.reshape(S // tm, tl, dil, LANES).transpose(0, 2, 1, 3).reshape(S, LANES) for t in rope]
    tab = pl.BlockSpec((tm, LANES), lambda b, i: (i, 0))
    sizes = [nq, nk, nv] + ([nx] if with_xq else [])
    scratch = [pltpu.VMEM((tm, D), BF16)]
    if dil > 1:
        scratch.insert(0, pltpu.VMEM((D // LANES, tm, LANES), F32))
    return pl.pallas_call(
        functools.partial(_inproj_kernel, nq, nk, nv, nx),
        grid=(B, S // tm),
        in_specs=[pl.BlockSpec((1, tm, D), lambda b, i: (b, i, 0)), _full(g.shape), _full(w.shape),
                  _full(ones.shape), tab, tab, tab, _full(qg.shape), _full(kg.shape), _full(xg.shape)],
        out_specs=[pl.BlockSpec((1, dil, n, tl, LANES), lambda b, i: (b, 0, 0, i, 0)) for n in sizes],
        out_shape=[jax.ShapeDtypeStruct((B, dil, n, S // dil, LANES), BF16) for n in sizes],
        scratch_shapes=scratch,
        compiler_params=_params("parallel", "parallel"),
        name="inproj",
    )(x, g, w, ones, *rope, qg, kg, xg)


def _exp2_rows(s):
    e = jnp.exp2(s - jnp.max(s, axis=-1, keepdims=True))
    return e.astype(BF16), jnp.sum(e, axis=-1, keepdims=True)


def _diff_attn_kernel(lam_init, shifted, q_ref, k_ref, v_ref, lam_ref, sg_ref, o_ref, s_ref):
    seq = k_ref.shape[2]
    ring, _, rows, _ = s_ref.shape
    kc = KEY_CHUNK
    n_sub = q_ref.shape[2] // rows
    lp = lam_ref[...]
    lam = (jnp.exp(jnp.sum(lp[0:1] * lp[1:2], axis=-1, keepdims=True))
           - jnp.exp(jnp.sum(lp[2:3] * lp[3:4], axis=-1, keepdims=True)) + lam_init)
    wide = lambda col: jnp.broadcast_to(col, (rows, LANES))
    lag = 1 if shifted else 0
    row_max, coef = {}, {}
    for it in range(n_sub + lag + 1):
        sb_qk, sb_exp, sb_pv = it, it - lag, it - lag - 1
        do_qk, do_exp, do_pv = sb_qk < n_sub, 0 <= sb_exp < n_sub, 0 <= sb_pv
        if do_qk:
            qs = _split_heads(q_ref[0, 0, sb_qk * rows:(sb_qk + 1) * rows, :])
            m_wide = [None, None]
        if do_exp:
            l_wide = [None, None]
        acc = None
        for c0 in range(0, seq, kc):
            cols = slice(c0, c0 + kc)
            if do_qk:
                for h in range(2):
                    s = _scores(qs[h], k_ref[0, 0, cols, :])
                    if shifted:
                        s_ref[sb_qk % ring, h, :, cols] = s
                        m_wide[h] = functools.reduce(
                            jnp.maximum, _lane_tiles(s) + ([] if m_wide[h] is None else [m_wide[h]]))
                    else:
                        e = [jnp.exp2(t) for t in _lane_tiles(s)]
                        s_ref[sb_qk % ring, h, :, cols] = jnp.concatenate(e, axis=1)
                        l_wide[h] = functools.reduce(jnp.add, e + ([] if l_wide[h] is None else [l_wide[h]]))
            if shifted and do_exp:
                for h in range(2):
                    e = [jnp.exp2(t - row_max[sb_exp][h]) for t in _lane_tiles(s_ref[sb_exp % ring, h, :, cols])]
                    s_ref[sb_exp % ring, h, :, cols] = jnp.concatenate(e, axis=1)
                    l_wide[h] = functools.reduce(jnp.add, e + ([] if l_wide[h] is None else [l_wide[h]]))
            if do_pv:
                c1, c2 = coef[sb_pv]
                diff = [t1 * c1 - t2 * c2 for t1, t2 in zip(_lane_tiles(s_ref[sb_pv % ring, 0, :, cols]),
                                                           _lane_tiles(s_ref[sb_pv % ring, 1, :, cols]))]
                pv = jnp.dot(jnp.concatenate(diff, axis=1).astype(BF16), v_ref[0, 0, cols, :],
                             preferred_element_type=F32)
                acc = pv if acc is None else acc + pv
        if shifted and do_qk:
            row_max[sb_qk] = [wide(jnp.max(m, axis=-1, keepdims=True)) for m in m_wide]
        if do_exp:
            l1, l2 = [jnp.sum(l, axis=-1, keepdims=True) for l in l_wide]
            coef[sb_exp] = (wide(1.0 / l1), wide(lam / l2))
        if do_pv:
            o_ref[0, sb_pv * rows:(sb_pv + 1) * rows, :] = _rms_rows(acc, sg_ref[...]).astype(BF16)


def _diff_attn_call(shifted, lam_init, q, k, v, lam_p, subln):
    B, H, S, _ = q.shape
    tq = Q_TILE_A
    kv = pl.BlockSpec((1, 1, S, LANES), lambda b, h, i: (b, h, 0, 0))
    stages_in_flight = 3 if shifted else 2
    return pl.pallas_call(
        functools.partial(_diff_attn_kernel, lam_init, shifted),
        grid=(B, H, S // tq),
        in_specs=[pl.BlockSpec((1, 1, tq, LANES), lambda b, h, i: (b, h, i, 0)), kv, kv,
                  _full(lam_p.shape), _full(subln.shape)],
        out_specs=pl.BlockSpec((1, tq, LANES), lambda b, h, i: (b, i, h)),
        out_shape=jax.ShapeDtypeStruct((B, S, H * LANES), BF16),
        scratch_shapes=[pltpu.VMEM((stages_in_flight, 2, Q_SUB_A, S), F32)],
        compiler_params=_params("parallel", "parallel", "parallel"),
        name="diff_attn_shifted" if shifted else "diff_attn",
    )(q, k, v, lam_p, subln)


def _diff_attn(q, k, v, lam_p, subln, lam_init, score_bound):
    return lax.cond(score_bound <= SAFE_EXP2_RANGE,
                    functools.partial(_diff_attn_call, False, lam_init),
                    functools.partial(_diff_attn_call, True, lam_init),
                    q, k, v, lam_p, subln)


def _band_attn_kernel(radius, q_ref, k_ref, v_ref, o_ref, l_ref):
    _, res, pairs, rows, _ = q_ref.shape
    L = k_ref.shape[3]
    tq = min(Q_TILE_B, L)
    span = min(L, tq + 2 * radius)
    row0 = pl.program_id(2) * rows
    col = lax.broadcasted_iota(jnp.int32, (tq, span), 1)
    qrow = lax.broadcasted_iota(jnp.int32, (tq, span), 0)
    for r in range(res):
        for blk in range(rows // tq):
            q0 = row0 + blk * tq
            start = pl.multiple_of(jnp.clip(q0 - radius, 0, L - span), radius)
            valid = jnp.abs(col - qrow + (start - q0)) <= radius
            rs = slice(blk * tq, (blk + 1) * tq)
            scores = [[_scores(qh, k_ref[0, r, p, pl.ds(start, span), :])
                       for qh in _split_heads(q_ref[0, r, p, rs, :])] for p in range(pairs)]
            probs = []
            for p in range(pairs):
                per_head = []
                for s in scores[p]:
                    s = jnp.where(valid, s, NEG_INF)
                    m = jnp.max(s, axis=-1, keepdims=True)
                    e = jnp.exp2(s - m)
                    z = jnp.sum(e, axis=-1, keepdims=True)
                    per_head.append((e.astype(BF16), z, m + jnp.log2(z)))
                probs.append(per_head)
            for p in range(pairs):
                vw = v_ref[0, r, p, pl.ds(start, span), :]
                outs = [jnp.dot(e, vw, preferred_element_type=F32) / z for e, z, _ in probs[p]]
                lses = [jnp.broadcast_to(lse, (tq, LANES)) for _, _, lse in probs[p]]
                o_ref[0, r, p, rs, :] = _merge_heads(*outs).astype(BF16)
                l_ref[0, r, p, rs, :] = _merge_heads(*lses)


def _band_attn(q, k, v, radius):
    B, dil, pairs, L, _ = q.shape
    rows = min(L, ROW_TILE)
    res = ROW_TILE // rows
    kv = pl.BlockSpec((1, res, pairs, L, LANES), lambda b, r, i: (b, r, 0, 0, 0))
    tile = pl.BlockSpec((1, res, pairs, rows, LANES), lambda b, r, i: (b, r, 0, i, 0))
    return pl.pallas_call(
        functools.partial(_band_attn_kernel, radius),
        grid=(B, dil // res, L // rows),
        in_specs=[tile, kv, kv],
        out_specs=[tile, tile],
        out_shape=[jax.ShapeDtypeStruct(q.shape, BF16), jax.ShapeDtypeStruct(q.shape, F32)],
        compiler_params=_params("parallel", "parallel", "parallel"),
        name="band_attn",
    )(q, k, v)


def _group_mix_into(o_refs, l_refs, stage_ref, dst_ref):
    def token_order(ref, p, slot):
        dil, tl = ref.shape[1], ref.shape[3]
        if dil == 1:
            return ref[0, 0, p].astype(F32)
        for r in range(dil):
            stage_ref[slot, pl.ds(r, tl, stride=dil), :] = ref[0, r, p].astype(F32)
        return stage_ref[slot]

    for p in range(o_refs[0].shape[2]):
        ls = [token_order(l, p, 2 * g) for g, l in enumerate(l_refs)]
        os_ = [token_order(o, p, 2 * g + 1) for g, o in enumerate(o_refs)]
        m = functools.reduce(jnp.maximum, ls)
        ws = [jnp.exp2(l - m) for l in ls]
        num = functools.reduce(jnp.add, [w * o for w, o in zip(ws, os_)])
        dst_ref[:, p * LANES:(p + 1) * LANES] = (num / functools.reduce(jnp.add, ws)).astype(BF16)


def _mem_kv_kernel(m_ref, g_ref, w_ref, ones_ref, kg_ref, k_ref, v_ref):
    h = _rms_rows(m_ref[0], g_ref[...]).astype(BF16)
    kv = jnp.dot(h, w_ref[...], preferred_element_type=F32)
    half = kv.shape[1] // 2
    for p in range(half // LANES):
        y = _pair_norm(kv[:, p * LANES:(p + 1) * LANES], ones_ref[...], kg_ref[...])
        k_ref[0, :, p * LANES:(p + 1) * LANES] = y.astype(BF16)
    v_ref[0] = kv[:, half:].astype(BF16)


def _mem_kv(mem, g, w, ones, kg):
    B, M, D = mem.shape
    width = w.shape[1] // 2
    blk = pl.BlockSpec((1, M, width), lambda b: (b, 0, 0))
    return pl.pallas_call(
        _mem_kv_kernel,
        grid=(B,),
        in_specs=[pl.BlockSpec((1, M, D), lambda b: (b, 0, 0)), _full(g.shape), _full(w.shape),
                  _full(ones.shape), _full(kg.shape)],
        out_specs=[blk, blk],
        out_shape=[jax.ShapeDtypeStruct((B, M, width), BF16)] * 2,
        compiler_params=_params("parallel"),
        name="mem_kv",
    )(mem, g, w, ones, kg)


def _cross_attn_into(q_ref, k_ref, v_ref, dst_ref, col0):
    pairs = range(q_ref.shape[2])
    cols = [slice(p * LANES, (p + 1) * LANES) for p in pairs]
    scores = [[_scores(qh, k_ref[0, :, cols[p]]) for qh in _split_heads(q_ref[0, 0, p])] for p in pairs]
    probs = [[_exp2_rows(s) for s in scores[p]] for p in pairs]
    for p in pairs:
        v = v_ref[0, :, cols[p]]
        outs = [jnp.dot(e, v, preferred_element_type=F32) / l for e, l in probs[p]]
        dst_ref[:, col0 + p * LANES:col0 + (p + 1) * LANES] = _merge_heads(*outs).astype(BF16)


def _outproj_kernel(n_groups, *refs):
    if n_groups:
        o_refs, l_refs = refs[:n_groups], refs[n_groups:2 * n_groups]
        xq_ref, mk_ref, mv_ref, x_ref, w_ref, out_ref, lhs_ref, stage_ref = refs[2 * n_groups:]
        mix_width = o_refs[0].shape[2] * LANES
        _group_mix_into(o_refs, l_refs, stage_ref, lhs_ref)
        _cross_attn_into(xq_ref, mk_ref, mv_ref, lhs_ref, mix_width)
        out_ref[0] = x_ref[0] + jnp.dot(lhs_ref[...], w_ref[...], preferred_element_type=F32)
    else:
        mix_ref, xq_ref, mk_ref, mv_ref, x_ref, w_ref, out_ref, lhs_ref = refs
        mix_width = mix_ref.shape[2]
        _cross_attn_into(xq_ref, mk_ref, mv_ref, lhs_ref, 0)
        out_ref[0] = (x_ref[0]
                      + jnp.dot(mix_ref[0], w_ref[:mix_width, :], preferred_element_type=F32)
                      + jnp.dot(lhs_ref[...], w_ref[mix_width:, :], preferred_element_type=F32))


def _outproj(mixed, xq, mk, mv, x, w):
    B, S, D = x.shape
    M, cross_width = mk.shape[1:]
    tm = ROW_TILE
    row = lambda c: pl.BlockSpec((1, tm, c), lambda b, i: (b, i, 0))
    kv = pl.BlockSpec((1, M, cross_width), lambda b, i: (b, 0, 0))
    common = [pl.BlockSpec((1, 1, xq.shape[2], tm, LANES), lambda b, i: (b, 0, 0, i, 0)), kv, kv,
              row(D), _full(w.shape)]
    if isinstance(mixed, tuple):
        groups = [*mixed[0], *mixed[1]]
        pairs = groups[0].shape[2]
        group_spec = lambda t: pl.BlockSpec((1, t.shape[1], pairs, tm // t.shape[1], LANES),
                                            lambda b, i: (b, 0, 0, i, 0))
        in_specs = [group_spec(t) for t in groups] + common
        args = groups
        scratch = [pltpu.VMEM((tm, pairs * LANES + cross_width), BF16),
                   pltpu.VMEM((len(groups), tm, LANES), F32)]
        n_groups = len(mixed[0])
    else:
        in_specs = [row(mixed.shape[2])] + common
        args = [mixed]
        scratch = [pltpu.VMEM((tm, cross_width), BF16)]
        n_groups = 0
    return pl.pallas_call(
        functools.partial(_outproj_kernel, n_groups),
        grid=(B, S // tm),
        in_specs=in_specs,
        out_specs=row(D),
        out_shape=jax.ShapeDtypeStruct((B, S, D), F32),
        scratch_shapes=scratch,
        compiler_params=_params("parallel", "parallel"),
        name="outproj",
    )(*args, xq, mk, mv, x, w)


FF_CHUNK = 256


HALO_ROWS = 16


def _ffn_kernel(x_ref, prev_ref, next_ref, g_ref, wu_ref, cw_ref, cb_ref, wd_ref, o_ref,
                h_ref, stage_ref, act_ref):
    tm = x_ref.shape[1]
    i, n = pl.program_id(1), pl.num_programs(1)
    g = g_ref[...]
    h_ref[0:HALO_ROWS] = (_rms_rows(prev_ref[0], g) * (i > 0).astype(F32)).astype(BF16)
    h_ref[HALO_ROWS:HALO_ROWS + tm] = _rms_rows(x_ref[0], g).astype(BF16)
    h_ref[HALO_ROWS + tm:] = (_rms_rows(next_ref[0], g) * (i < n - 1).astype(F32)).astype(BF16)
    tiles = FF_CHUNK // LANES

    def project(c0):
        return [jnp.dot(h_ref[...], wu_ref[:, col:col + FF_CHUNK], preferred_element_type=F32)
                for col in (c0, D_FF + c0)]

    def conv(u, col, slot):
        parts = []
        for t in range(tiles):
            stage_ref[slot, t] = u[:, t * LANES:(t + 1) * LANES]
            cols = slice(col + t * LANES, col + (t + 1) * LANES)
            taps = [stage_ref[slot, t, pl.ds(HALO_ROWS - 1 + k, tm), :] * cw_ref[k:k + 1, cols]
                    for k in range(3)]
            parts.append(taps[0] + taps[1] + taps[2] + cb_ref[:, cols])
        return jnp.concatenate(parts, axis=1)

    starts = list(range(0, D_FF, FF_CHUNK))
    u_next = project(starts[0])
    for idx, c0 in enumerate(starts):
        u_a, u_b = u_next
        if idx + 1 < len(starts):
            u_next = project(starts[idx + 1])
        slot = 2 * (idx % 2)
        a, b = conv(u_a, c0, slot), conv(u_b, D_FF + c0, slot + 1)
        act_ref[:, c0:c0 + FF_CHUNK] = (a / (1.0 + jnp.exp(-a)) * b).astype(BF16)
    o_ref[0] = x_ref[0] + jnp.dot(act_ref[...], wd_ref[...], preferred_element_type=F32)


def _ffn(x, g, w_up, conv_w, conv_b, w_down):
    B, S, D = x.shape
    tm = FFN_ROW_TILE
    halo_per_tile = tm // HALO_ROWS
    last_halo = S // HALO_ROWS - 1
    tile = pl.BlockSpec((1, tm, D), lambda b, i: (b, i, 0))
    return pl.pallas_call(
        _ffn_kernel,
        grid=(B, S // tm),
        in_specs=[tile,
                  pl.BlockSpec((1, HALO_ROWS, D), lambda b, i: (b, jnp.maximum(i * halo_per_tile - 1, 0), 0)),
                  pl.BlockSpec((1, HALO_ROWS, D),
                               lambda b, i: (b, jnp.minimum((i + 1) * halo_per_tile, last_halo), 0)),
                  _full(g.shape), _resident(w_up.shape), _full(conv_w.shape), _full(conv_b.shape),
                  _resident(w_down.shape)],
        out_specs=tile,
        out_shape=jax.ShapeDtypeStruct((B, S, D), F32),
        scratch_shapes=[pltpu.VMEM((tm + 2 * HALO_ROWS, D), BF16),
                        pltpu.VMEM((4, FF_CHUNK // LANES, tm + 2 * HALO_ROWS, LANES), F32),
                        pltpu.VMEM((tm, D_FF), BF16)],
        compiler_params=_params("parallel", "parallel"),
        name="ffn",
    )(x, x, x, g, w_up, conv_w, conv_b, w_down)


def _rope_tables(S):
    inv = ROPE_THETA ** (-(jnp.arange(ROT_HALF, dtype=F32) * 2.0 / (2 * ROT_HALF)))
    ang = jnp.arange(S, dtype=F32)[:, None] * inv[None, :]
    cos, sin = jnp.cos(ang), jnp.sin(ang)
    rest = HEAD_DIM - 2 * ROT_HALF
    one, zero = jnp.ones((S, rest), F32), jnp.zeros((S, rest), F32)
    zhalf = jnp.zeros((S, ROT_HALF), F32)
    per_head = (jnp.concatenate([cos, cos, one], axis=1),
                jnp.concatenate([zhalf, sin, zero], axis=1),
                jnp.concatenate([-sin, zhalf, zero], axis=1))
    return tuple(jnp.tile(t, (1, LANES // HEAD_DIM)) for t in per_head)


def _pair_gain(g, scale=1.0):
    g = jnp.atleast_2d(g.astype(F32) * scale)
    return jnp.tile(g, (1, LANES // HEAD_DIM))


def _group_ones(width):
    head = jnp.arange(width) // HEAD_DIM
    return (head[:, None] == head[None, :]).astype(BF16)


def _a_weight(w):
    d = w.shape[0]
    n = A_HEADS * HEAD_DIM
    pair = lambda t: t.reshape(d, 2, A_HEADS, HEAD_DIM).transpose(0, 2, 1, 3).reshape(d, 2 * n)
    return jnp.concatenate([pair(w[:, :2 * n]), pair(w[:, 2 * n:4 * n]), w[:, 4 * n:]], axis=1).astype(BF16)


def _trunk(x, mem, P):
    S = x.shape[1]
    rope = _rope_tables(S)
    ones = _group_ones(MXU_COLS)
    n_layers = P["norm_mix"].shape[0]
    for i in range(n_layers):
        j = i // 2
        row = lambda name: P[name][i][None, :].astype(F32)
        mk, mv = _mem_kv(mem, row("norm_mem"), P["w_mem_kv"][i].astype(BF16), _group_ones(LANES),
                         _pair_gain(P["xk_norm"][i]))
        xg = _pair_gain(P["xq_norm"][i], Q_SCALE)
        if i % 2 == 0:
            qg = jnp.tile(_pair_gain(P["a_q_norm"][j], Q_SCALE), (A_HEADS, 1))
            kg = jnp.tile(_pair_gain(P["a_k_norm"][j]), (A_HEADS, 1))
            q, k, v, xq = _inproj(x, row("norm_mix"), _a_weight(P["a_w_in"][j]), ones, rope,
                                  qg, kg, xg, A_HEADS)
            lam_init = 0.8 - 0.6 * math.exp(-0.3 * i)
            subln = P["a_subln"][j][None, :].astype(F32) * (1.0 - lam_init)
            score_bound = (NORM_SLACK * HEAD_DIM * Q_SCALE * jnp.max(jnp.abs(P["a_q_norm"][j]))
                           * jnp.max(jnp.abs(P["a_k_norm"][j])))
            mixed = _diff_attn(q[:, 0], k[:, 0], v[:, 0], P["a_lambda"][j].astype(F32), subln, lam_init,
                               score_bound)
            w_out = P["a_w_out"][j]
        else:
            pairs = B_HEADS // 2
            width = B_HEADS * HEAD_DIM
            w_in = P["b_w_in"][j]
            outs, lses = [], []
            for gi, (window, dil) in enumerate(B_GROUPS):
                cols = [w_in[:, s * len(B_GROUPS) * width + gi * width:][:, :width] for s in range(3)]
                if gi == 0:
                    cols.append(w_in[:, 3 * len(B_GROUPS) * width:])
                qg = jnp.tile(_pair_gain(P["b_q_norm"][j][gi], Q_SCALE), (pairs, 1))
                kg = jnp.tile(_pair_gain(P["b_k_norm"][j][gi]), (pairs, 1))
                q, k, v, *rest = _inproj(x, row("norm_mix"), jnp.concatenate(cols, axis=1).astype(BF16),
                                         ones, rope, qg, kg, xg, pairs, dil=dil, with_xq=(gi == 0))
                if gi == 0:
                    xq = rest[0]
                o, l = _band_attn(q, k, v, (window // 2) // dil)
                outs.append(o)
                lses.append(l)
            mixed = (outs, lses)
            w_out = P["b_w_out"][j]
        x = _outproj(mixed, xq, mk, mv, x, w_out.astype(BF16))
        x = _ffn(x, row("norm_ffn"), P["w_up"][i].astype(BF16), P["conv_w"][i].astype(F32),
                 P["conv_b"][i][None, :].astype(F32), P["w_down"][i].astype(BF16))
    return x


def kernel(x_prompt, x_sample, mem_prompt, mem_sample, norm_mix, norm_mem, w_mem_kv, xq_norm, xk_norm, a_w_in, a_w_out, a_q_norm, a_k_norm, a_lambda, a_subln, b_w_in, b_w_out, b_q_norm, b_k_norm, norm_ffn, w_up, conv_w, conv_b, w_down):
    P = dict(norm_mix=norm_mix, norm_mem=norm_mem, w_mem_kv=w_mem_kv, xq_norm=xq_norm, xk_norm=xk_norm,
             a_w_in=a_w_in, a_w_out=a_w_out, a_q_norm=a_q_norm, a_k_norm=a_k_norm, a_lambda=a_lambda,
             a_subln=a_subln, b_w_in=b_w_in, b_w_out=b_w_out, b_q_norm=b_q_norm, b_k_norm=b_k_norm,
             norm_ffn=norm_ffn, w_up=w_up, conv_w=conv_w, conv_b=conv_b, w_down=w_down)
    return _trunk(x_prompt, mem_prompt, P), _trunk(x_sample, mem_sample, P)
```

```python
import functools
import math

import jax
import jax.numpy as jnp
from jax import lax
from jax.experimental import pallas as pl
from jax.experimental.pallas import tpu as pltpu

F32 = jnp.float32
BF16 = jnp.bfloat16

LANES = 128
HEAD_DIM = 64
ROT_HALF = 8
A_HEADS = 8
B_HEADS = 8
B_GROUPS = ((128, 1), (512, 4), (2048, 16))
X_HEADS = 4
D_FF = 2816
ROPE_THETA = 500000.0
EPS = 1e-6
NEG_INF = -1e30
LOG2E = math.log2(math.e)
Q_SCALE = LOG2E / math.sqrt(HEAD_DIM)

V7X_VMEM_BYTES = 64 * 1024 * 1024
VMEM_LIMIT = V7X_VMEM_BYTES * 7 // 8

ROW_TILE = 512
INPROJ_ROW_TILE = 1024
OUTPROJ_ROW_TILE = 1024
FFN_ROW_TILE = 512
Q_TILE_A = 2048
Q_SUB_A = 256
KEY_CHUNK = 512
SAFE_EXP2_RANGE = 50.0
NORM_SLACK = 1.02
Q_TILE_B = 128
MXU_COLS = 256


def _params(*sem):
    return pltpu.CompilerParams(dimension_semantics=sem, vmem_limit_bytes=VMEM_LIMIT)


def _full(shape):
    return pl.BlockSpec(shape, lambda *_: (0,) * len(shape))


def _resident(shape):
    return pl.BlockSpec(shape, lambda *_: (0,) * len(shape), pipeline_mode=pl.Buffered(1))


def _rms_rows(x, g):
    ms = jnp.mean(x * x, axis=-1, keepdims=True)
    return x * lax.rsqrt(ms + EPS) * g


def _lane_tiles(x):
    return [x[:, t * LANES:(t + 1) * LANES] for t in range(x.shape[1] // LANES)]


def _scale_rows(y, ssq, gain):
    return y * lax.rsqrt(ssq * (1.0 / HEAD_DIM) + EPS) * gain


def _pair_norm(y, group_ones, gain):
    ssq = jnp.dot((y * y).astype(BF16), group_ones, preferred_element_type=F32)
    return _scale_rows(y, ssq, gain)


def _rope(y, cos, sin_lo, sin_hi):
    return (y * cos + pltpu.roll(y, ROT_HALF, axis=1) * sin_lo
            + pltpu.roll(y, LANES - ROT_HALF, axis=1) * sin_hi)


def _split_heads(q):
    lane = lax.broadcasted_iota(jnp.int32, q.shape, 1)
    zero = jnp.zeros_like(q)
    return jnp.where(lane < HEAD_DIM, q, zero), jnp.where(lane >= HEAD_DIM, q, zero)


def _merge_heads(o_lo, o_hi):
    lane = lax.broadcasted_iota(jnp.int32, o_lo.shape, 1)
    return jnp.where(lane < HEAD_DIM, o_lo, o_hi)


def _scores(q, k):
    return lax.dot_general(q, k, (((1,), (1,)), ((), ())), preferred_element_type=F32)


def _inproj_kernel(nq, nk, nv, nx, x_ref, g_ref, w_ref, ones_ref, cos_ref, slo_ref, shi_ref,
                   qg_ref, kg_ref, xg_ref, *out_and_scratch):
    q_ref, k_ref, v_ref = out_and_scratch[:3]
    dil, tl = q_ref.shape[1], q_ref.shape[3]
    if dil == 1:
        h_ref = out_and_scratch[-1]
        h_ref[...] = _rms_rows(x_ref[0], g_ref[...]).astype(BF16)
    else:
        stage_ref, h_ref = out_and_scratch[-2:]
        h = _rms_rows(x_ref[0], g_ref[...])
        for c in range(stage_ref.shape[0]):
            stage_ref[c] = h[:, c * LANES:(c + 1) * LANES]
        for r in range(dil):
            for c in range(stage_ref.shape[0]):
                h_ref[r * tl:(r + 1) * tl, c * LANES:(c + 1) * LANES] = (
                    stage_ref[c, pl.ds(r, tl, stride=dil), :].astype(BF16))

    def put(ref, p, y):
        for r in range(dil):
            ref[0, r, p] = y[r * tl:(r + 1) * tl].astype(BF16)

    ones = ones_ref[...]
    cos, slo, shi = cos_ref[...], slo_ref[...], shi_ref[...]
    pairs_per_chunk = MXU_COLS // LANES
    n_chunks = (nq + nk + nv + nx) // pairs_per_chunk

    def project(ch):
        return jnp.dot(h_ref[...], w_ref[:, ch * MXU_COLS:(ch + 1) * MXU_COLS],
                       preferred_element_type=F32)

    acc_next = project(0)
    for ch in range(n_chunks):
        acc = acc_next
        if ch + 1 < n_chunks:
            acc_next = project(ch + 1)
        ys = _lane_tiles(acc)
        first = ch * pairs_per_chunk
        if first < nq + nk or first >= nq + nk + nv:
            ssq = _lane_tiles(jnp.dot((acc * acc).astype(BF16), ones, preferred_element_type=F32))
        for half, y in enumerate(ys):
            p = first + half
            if p < nq + nk:
                ref, gain = (q_ref, qg_ref) if p < nq else (k_ref, kg_ref)
                p = p if p < nq else p - nq
                put(ref, p, _rope(_scale_rows(y, ssq[half], gain[p:p + 1, :]), cos, slo, shi))
            elif p < nq + nk + nv:
                put(v_ref, p - nq - nk, y)
            else:
                put(out_and_scratch[3], p - nq - nk - nv, _scale_rows(y, ssq[half], xg_ref[...]))


def _inproj(x, g, w, ones, rope, qg, kg, xg, nv, dil=1, with_xq=True):
    B, S, D = x.shape
    nq, nk, nx = qg.shape[0], kg.shape[0], (X_HEADS // 2 if with_xq else 0)
    tm = INPROJ_ROW_TILE
    tl = tm // dil
    rope = [t.reshape(S // tm, tl, dil, LANES).transpose(0, 2, 1, 3).reshape(S, LANES) for t in rope]
    tab = pl.BlockSpec((tm, LANES), lambda b, i: (i, 0))
    sizes = [nq, nk, nv] + ([nx] if with_xq else [])
    scratch = [pltpu.VMEM((tm, D), BF16)]
    if dil > 1:
        scratch.insert(0, pltpu.VMEM((D // LANES, tm, LANES), F32))
    return pl.pallas_call(
        functools.partial(_inproj_kernel, nq, nk, nv, nx),
        grid=(B, S // tm),
        in_specs=[pl.BlockSpec((1, tm, D), lambda b, i: (b, i, 0)), _full(g.shape), _full(w.shape),
                  _full(ones.shape), tab, tab, tab, _full(qg.shape), _full(kg.shape), _full(xg.shape)],
        out_specs=[pl.BlockSpec((1, dil, n, tl, LANES), lambda b, i: (b, 0, 0, i, 0)) for n in sizes],
        out_shape=[jax.ShapeDtypeStruct((B, dil, n, S // dil, LANES), BF16) for n in sizes],
        scratch_shapes=scratch,
        compiler_params=_params("parallel", "parallel"),
        name="inproj",
    )(x, g, w, ones, *rope, qg, kg, xg)


def _exp2_rows(s):
    e = jnp.exp2(s - jnp.max(s, axis=-1, keepdims=True))
    return e.astype(BF16), jnp.sum(e, axis=-1, keepdims=True)


def _diff_attn_kernel(lam_init, shifted, q_ref, k_ref, v_ref, lam_ref, sg_ref, o_ref, s_ref):
    seq = k_ref.shape[2]
    ring, _, rows, _ = s_ref.shape
    kc = KEY_CHUNK
    n_sub = q_ref.shape[2] // rows
    lp = lam_ref[...]
    lam = (jnp.exp(jnp.sum(lp[0:1] * lp[1:2], axis=-1, keepdims=True))
           - jnp.exp(jnp.sum(lp[2:3] * lp[3:4], axis=-1, keepdims=True)) + lam_init)
    wide = lambda col: jnp.broadcast_to(col, (rows, LANES))
    lag = 1 if shifted else 0
    row_max, coef = {}, {}
    for it in range(n_sub + lag + 1):
        sb_qk, sb_exp, sb_pv = it, it - lag, it - lag - 1
        do_qk, do_exp, do_pv = sb_qk < n_sub, 0 <= sb_exp < n_sub, 0 <= sb_pv
        if do_qk:
            qs = _split_heads(q_ref[0, 0, sb_qk * rows:(sb_qk + 1) * rows, :])
            m_wide = [None, None]
        if do_exp:
            l_wide = [None, None]
        acc = None
        for c0 in range(0, seq, kc):
            cols = slice(c0, c0 + kc)
            if do_qk:
                for h in range(2):
                    s = _scores(qs[h], k_ref[0, 0, cols, :])
                    if shifted:
                        s_ref[sb_qk % ring, h, :, cols] = s
                        m_wide[h] = functools.reduce(
                            jnp.maximum, _lane_tiles(s) + ([] if m_wide[h] is None else [m_wide[h]]))
                    else:
                        e = [jnp.exp2(t) for t in _lane_tiles(s)]
                        s_ref[sb_qk % ring, h, :, cols] = jnp.concatenate(e, axis=1)
                        l_wide[h] = functools.reduce(jnp.add, e + ([] if l_wide[h] is None else [l_wide[h]]))
            if shifted and do_exp:
                for h in range(2):
                    e = [jnp.exp2(t - row_max[sb_exp][h]) for t in _lane_tiles(s_ref[sb_exp % ring, h, :, cols])]
                    s_ref[sb_exp % ring, h, :, cols] = jnp.concatenate(e, axis=1)
                    l_wide[h] = functools.reduce(jnp.add, e + ([] if l_wide[h] is None else [l_wide[h]]))
            if do_pv:
                c1, c2 = coef[sb_pv]
                diff = [t1 * c1 - t2 * c2 for t1, t2 in zip(_lane_tiles(s_ref[sb_pv % ring, 0, :, cols]),
                                                           _lane_tiles(s_ref[sb_pv % ring, 1, :, cols]))]
                pv = jnp.dot(jnp.concatenate(diff, axis=1).astype(BF16), v_ref[0, 0, cols, :],
                             preferred_element_type=F32)
                acc = pv if acc is None else acc + pv
        if shifted and do_qk:
            row_max[sb_qk] = [wide(jnp.max(m, axis=-1, keepdims=True)) for m in m_wide]
        if do_exp:
            l1, l2 = [jnp.sum(l, axis=-1, keepdims=True) for l in l_wide]
            coef[sb_exp] = (wide(1.0 / l1), wide(lam / l2))
        if do_pv:
            o_ref[0, sb_pv * rows:(sb_pv + 1) * rows, :] = _rms_rows(acc, sg_ref[...]).astype(BF16)


def _diff_attn_call(shifted, lam_init, q, k, v, lam_p, subln):
    B, H, S, _ = q.shape
    tq = Q_TILE_A
    kv = pl.BlockSpec((1, 1, S, LANES), lambda b, h, i: (b, h, 0, 0))
    stages_in_flight = 3 if shifted else 2
    return pl.pallas_call(
        functools.partial(_diff_attn_kernel, lam_init, shifted),
        grid=(B, H, S // tq),
        in_specs=[pl.BlockSpec((1, 1, tq, LANES), lambda b, h, i: (b, h, i, 0)), kv, kv,
                  _full(lam_p.shape), _full(subln.shape)],
        out_specs=pl.BlockSpec((1, tq, LANES), lambda b, h, i: (b, i, h)),
        out_shape=jax.ShapeDtypeStruct((B, S, H * LANES), BF16),
        scratch_shapes=[pltpu.VMEM((stages_in_flight, 2, Q_SUB_A, S), F32)],
        compiler_params=_params("parallel", "parallel", "parallel"),
        name="diff_attn_shifted" if shifted else "diff_attn",
    )(q, k, v, lam_p, subln)


def _diff_attn(q, k, v, lam_p, subln, lam_init, score_bound):
    return lax.cond(score_bound <= SAFE_EXP2_RANGE,
                    functools.partial(_diff_attn_call, False, lam_init),
                    functools.partial(_diff_attn_call, True, lam_init),
                    q, k, v, lam_p, subln)


def _band_attn_kernel(radius, q_ref, k_ref, v_ref, o_ref, l_ref):
    _, res, pairs, rows, _ = q_ref.shape
    L = k_ref.shape[3]
    tq = min(Q_TILE_B, L)
    span = min(L, tq + 2 * radius)
    row0 = pl.program_id(2) * rows
    col = lax.broadcasted_iota(jnp.int32, (tq, span), 1)
    qrow = lax.broadcasted_iota(jnp.int32, (tq, span), 0)
    for r in range(res):
        for blk in range(rows // tq):
            q0 = row0 + blk * tq
            start = pl.multiple_of(jnp.clip(q0 - radius, 0, L - span), radius)
            valid = jnp.abs(col - qrow + (start - q0)) <= radius
            rs = slice(blk * tq, (blk + 1) * tq)
            scores = [[_scores(qh, k_ref[0, r, p, pl.ds(start, span), :])
                       for qh in _split_heads(q_ref[0, r, p, rs, :])] for p in range(pairs)]
            probs = []
            for p in range(pairs):
                per_head = []
                for s in scores[p]:
                    s = jnp.where(valid, s, NEG_INF)
                    m = jnp.max(s, axis=-1, keepdims=True)
                    e = jnp.exp2(s - m)
                    z = jnp.sum(e, axis=-1, keepdims=True)
                    per_head.append((e.astype(BF16), z, m + jnp.log2(z)))
                probs.append(per_head)
            for p in range(pairs):
                vw = v_ref[0, r, p, pl.ds(start, span), :]
                outs = [jnp.dot(e, vw, preferred_element_type=F32) / z for e, z, _ in probs[p]]
                lses = [jnp.broadcast_to(lse, (tq, LANES)) for _, _, lse in probs[p]]
                o_ref[0, r, p, rs, :] = _merge_heads(*outs).astype(BF16)
                l_ref[0, r, p, rs, :] = _merge_heads(*lses)


def _band_attn(q, k, v, radius):
    B, dil, pairs, L, _ = q.shape
    rows = min(L, ROW_TILE)
    res = ROW_TILE // rows
    kv = pl.BlockSpec((1, res, pairs, L, LANES), lambda b, r, i: (b, r, 0, 0, 0))
    tile = pl.BlockSpec((1, res, pairs, rows, LANES), lambda b, r, i: (b, r, 0, i, 0))
    return pl.pallas_call(
        functools.partial(_band_attn_kernel, radius),
        grid=(B, dil // res, L // rows),
        in_specs=[tile, kv, kv],
        out_specs=[tile, tile],
        out_shape=[jax.ShapeDtypeStruct(q.shape, BF16), jax.ShapeDtypeStruct(q.shape, F32)],
        compiler_params=_params("parallel", "parallel", "parallel"),
        name="band_attn",
    )(q, k, v)


def _group_mix_into(o_refs, l_refs, stage_ref, dst_ref):
    def token_order(ref, p, slot):
        dil, tl = ref.shape[1], ref.shape[3]
        if dil == 1:
            return ref[0, 0, p].astype(F32)
        for r in range(dil):
            stage_ref[slot, pl.ds(r, tl, stride=dil), :] = ref[0, r, p].astype(F32)
        return stage_ref[slot]

    for p in range(o_refs[0].shape[2]):
        ls = [token_order(l, p, 2 * g) for g, l in enumerate(l_refs)]
        os_ = [token_order(o, p, 2 * g + 1) for g, o in enumerate(o_refs)]
        m = functools.reduce(jnp.maximum, ls)
        ws = [jnp.exp2(l - m) for l in ls]
        num = functools.reduce(jnp.add, [w * o for w, o in zip(ws, os_)])
        dst_ref[:, p * LANES:(p + 1) * LANES] = (num / functools.reduce(jnp.add, ws)).astype(BF16)


def _mem_kv_kernel(m_ref, g_ref, w_ref, ones_ref, kg_ref, k_ref, v_ref):
    h = _rms_rows(m_ref[0], g_ref[...]).astype(BF16)
    kv = jnp.dot(h, w_ref[...], preferred_element_type=F32)
    half = kv.shape[1] // 2
    for p in range(half // LANES):
        y = _pair_norm(kv[:, p * LANES:(p + 1) * LANES], ones_ref[...], kg_ref[...])
        k_ref[0, :, p * LANES:(p + 1) * LANES] = y.astype(BF16)
    v_ref[0] = kv[:, half:].astype(BF16)


def _mem_kv(mem, g, w, ones, kg):
    B, M, D = mem.shape
    width = w.shape[1] // 2
    blk = pl.BlockSpec((1, M, width), lambda b: (b, 0, 0))
    return pl.pallas_call(
        _mem_kv_kernel,
        grid=(B,),
        in_specs=[pl.BlockSpec((1, M, D), lambda b: (b, 0, 0)), _full(g.shape), _full(w.shape),
                  _full(ones.shape), _full(kg.shape)],
        out_specs=[blk, blk],
        out_shape=[jax.ShapeDtypeStruct((B, M, width), BF16)] * 2,
        compiler_params=_params("parallel"),
        name="mem_kv",
    )(mem, g, w, ones, kg)


def _cross_attn_into(q_ref, k_ref, v_ref, dst_ref, col0):
    pairs = range(q_ref.shape[2])
    cols = [slice(p * LANES, (p + 1) * LANES) for p in pairs]
    scores = [[_scores(qh, k_ref[0, :, cols[p]]) for qh in _split_heads(q_ref[0, 0, p])] for p in pairs]
    probs = [[_exp2_rows(s) for s in scores[p]] for p in pairs]
    for p in pairs:
        v = v_ref[0, :, cols[p]]
        outs = [jnp.dot(e, v, preferred_element_type=F32) / l for e, l in probs[p]]
        dst_ref[:, col0 + p * LANES:col0 + (p + 1) * LANES] = _merge_heads(*outs).astype(BF16)


def _outproj_kernel(n_groups, *refs):
    if n_groups:
        o_refs, l_refs = refs[:n_groups], refs[n_groups:2 * n_groups]
        xq_ref, mk_ref, mv_ref, x_ref, w_ref, out_ref, lhs_ref, stage_ref = refs[2 * n_groups:]
        mix_width = o_refs[0].shape[2] * LANES
        _group_mix_into(o_refs, l_refs, stage_ref, lhs_ref)
        _cross_attn_into(xq_ref, mk_ref, mv_ref, lhs_ref, mix_width)
        out_ref[0] = x_ref[0] + jnp.dot(lhs_ref[...], w_ref[...], preferred_element_type=F32)
    else:
        mix_ref, xq_ref, mk_ref, mv_ref, x_ref, w_ref, out_ref, lhs_ref = refs
        mix_width = mix_ref.shape[2]
        _cross_attn_into(xq_ref, mk_ref, mv_ref, lhs_ref, 0)
        out_ref[0] = (x_ref[0]
                      + jnp.dot(mix_ref[0], w_ref[:mix_width, :], preferred_element_type=F32)
                      + jnp.dot(lhs_ref[...], w_ref[mix_width:, :], preferred_element_type=F32))


def _outproj(mixed, xq, mk, mv, x, w):
    B, S, D = x.shape
    M, cross_width = mk.shape[1:]
    tm = OUTPROJ_ROW_TILE
    row = lambda c: pl.BlockSpec((1, tm, c), lambda b, i: (b, i, 0))
    kv = pl.BlockSpec((1, M, cross_width), lambda b, i: (b, 0, 0))
    common = [pl.BlockSpec((1, 1, xq.shape[2], tm, LANES), lambda b, i: (b, 0, 0, i, 0)), kv, kv,
              row(D), _full(w.shape)]
    if isinstance(mixed, tuple):
        groups = [*mixed[0], *mixed[1]]
        pairs = groups[0].shape[2]
        group_spec = lambda t: pl.BlockSpec((1, t.shape[1], pairs, tm // t.shape[1], LANES),
                                            lambda b, i: (b, 0, 0, i, 0))
        in_specs = [group_spec(t) for t in groups] + common
        args = groups
        scratch = [pltpu.VMEM((tm, pairs * LANES + cross_width), BF16),
                   pltpu.VMEM((len(groups), tm, LANES), F32)]
        n_groups = len(mixed[0])
    else:
        in_specs = [row(mixed.shape[2])] + common
        args = [mixed]
        scratch = [pltpu.VMEM((tm, cross_width), BF16)]
        n_groups = 0
    return pl.pallas_call(
        functools.partial(_outproj_kernel, n_groups),
        grid=(B, S // tm),
        in_specs=in_specs,
        out_specs=row(D),
        out_shape=jax.ShapeDtypeStruct((B, S, D), F32),
        scratch_shapes=scratch,
        compiler_params=_params("parallel", "parallel"),
        name="outproj",
    )(*args, xq, mk, mv, x, w)


FF_CHUNK = 256


HALO_ROWS = 16


def _ffn_kernel(x_ref, prev_ref, next_ref, g_ref, wu_ref, cw_ref, cb_ref, wd_ref, o_ref,
                h_ref, stage_ref, act_ref):
    tm = x_ref.shape[1]
    i, n = pl.program_id(1), pl.num_programs(1)
    g = g_ref[...]
    h_ref[0:HALO_ROWS] = (_rms_rows(prev_ref[0], g) * (i > 0).astype(F32)).astype(BF16)
    h_ref[HALO_ROWS:HALO_ROWS + tm] = _rms_rows(x_ref[0], g).astype(BF16)
    h_ref[HALO_ROWS + tm:] = (_rms_rows(next_ref[0], g) * (i < n - 1).astype(F32)).astype(BF16)
    tiles = FF_CHUNK // LANES

    def project(c0):
        return [jnp.dot(h_ref[...], wu_ref[:, col:col + FF_CHUNK], preferred_element_type=F32)
                for col in (c0, D_FF + c0)]

    def conv(u, col, slot):
        parts = []
        for t in range(tiles):
            stage_ref[slot, t] = u[:, t * LANES:(t + 1) * LANES]
            cols = slice(col + t * LANES, col + (t + 1) * LANES)
            taps = [stage_ref[slot, t, pl.ds(HALO_ROWS - 1 + k, tm), :] * cw_ref[k:k + 1, cols]
                    for k in range(3)]
            parts.append(taps[0] + taps[1] + taps[2] + cb_ref[:, cols])
        return jnp.concatenate(parts, axis=1)

    starts = list(range(0, D_FF, FF_CHUNK))
    u_next = project(starts[0])
    for idx, c0 in enumerate(starts):
        u_a, u_b = u_next
        if idx + 1 < len(starts):
            u_next = project(starts[idx + 1])
        slot = 2 * (idx % 2)
        a, b = conv(u_a, c0, slot), conv(u_b, D_FF + c0, slot + 1)
        act_ref[:, c0:c0 + FF_CHUNK] = (a / (1.0 + jnp.exp(-a)) * b).astype(BF16)
    o_ref[0] = x_ref[0] + jnp.dot(act_ref[...], wd_ref[...], preferred_element_type=F32)


def _ffn(x, g, w_up, conv_w, conv_b, w_down):
    B, S, D = x.shape
    tm = FFN_ROW_TILE
    halo_per_tile = tm // HALO_ROWS
    last_halo = S // HALO_ROWS - 1
    tile = pl.BlockSpec((1, tm, D), lambda b, i: (b, i, 0))
    return pl.pallas_call(
        _ffn_kernel,
        grid=(B, S // tm),
        in_specs=[tile,
                  pl.BlockSpec((1, HALO_ROWS, D), lambda b, i: (b, jnp.maximum(i * halo_per_tile - 1, 0), 0)),
                  pl.BlockSpec((1, HALO_ROWS, D),
                               lambda b, i: (b, jnp.minimum((i + 1) * halo_per_tile, last_halo), 0)),
                  _full(g.shape), _resident(w_up.shape), _full(conv_w.shape), _full(conv_b.shape),
                  _resident(w_down.shape)],
        out_specs=tile,
        out_shape=jax.ShapeDtypeStruct((B, S, D), F32),
        scratch_shapes=[pltpu.VMEM((tm + 2 * HALO_ROWS, D), BF16),
                        pltpu.VMEM((4, FF_CHUNK // LANES, tm + 2 * HALO_ROWS, LANES), F32),
                        pltpu.VMEM((tm, D_FF), BF16)],
        compiler_params=_params("parallel", "parallel"),
        name="ffn",
    )(x, x, x, g, w_up, conv_w, conv_b, w_down)


def _rope_tables(S):
    inv = ROPE_THETA ** (-(jnp.arange(ROT_HALF, dtype=F32) * 2.0 / (2 * ROT_HALF)))
    ang = jnp.arange(S, dtype=F32)[:, None] * inv[None, :]
    cos, sin = jnp.cos(ang), jnp.sin(ang)
    rest = HEAD_DIM - 2 * ROT_HALF
    one, zero = jnp.ones((S, rest), F32), jnp.zeros((S, rest), F32)
    zhalf = jnp.zeros((S, ROT_HALF), F32)
    per_head = (jnp.concatenate([cos, cos, one], axis=1),
                jnp.concatenate([zhalf, sin, zero], axis=1),
                jnp.concatenate([-sin, zhalf, zero], axis=1))
    return tuple(jnp.tile(t, (1, LANES // HEAD_DIM)) for t in per_head)


def _pair_gain(g, scale=1.0):
    g = jnp.atleast_2d(g.astype(F32) * scale)
    return jnp.tile(g, (1, LANES // HEAD_DIM))


def _group_ones(width):
    head = jnp.arange(width) // HEAD_DIM
    return (head[:, None] == head[None, :]).astype(BF16)


def _a_weight(w):
    d = w.shape[0]
    n = A_HEADS * HEAD_DIM
    pair = lambda t: t.reshape(d, 2, A_HEADS, HEAD_DIM).transpose(0, 2, 1, 3).reshape(d, 2 * n)
    return jnp.concatenate([pair(w[:, :2 * n]), pair(w[:, 2 * n:4 * n]), w[:, 4 * n:]], axis=1).astype(BF16)


def _trunk(x, mem, P):
    S = x.shape[1]
    rope = _rope_tables(S)
    ones = _group_ones(MXU_COLS)
    n_layers = P["norm_mix"].shape[0]
    for i in range(n_layers):
        j = i // 2
        row = lambda name: P[name][i][None, :].astype(F32)
        mk, mv = _mem_kv(mem, row("norm_mem"), P["w_mem_kv"][i].astype(BF16), _group_ones(LANES),
                         _pair_gain(P["xk_norm"][i]))
        xg = _pair_gain(P["xq_norm"][i], Q_SCALE)
        if i % 2 == 0:
            qg = jnp.tile(_pair_gain(P["a_q_norm"][j], Q_SCALE), (A_HEADS, 1))
            kg = jnp.tile(_pair_gain(P["a_k_norm"][j]), (A_HEADS, 1))
            q, k, v, xq = _inproj(x, row("norm_mix"), _a_weight(P["a_w_in"][j]), ones, rope,
                                  qg, kg, xg, A_HEADS)
            lam_init = 0.8 - 0.6 * math.exp(-0.3 * i)
            subln = P["a_subln"][j][None, :].astype(F32) * (1.0 - lam_init)
            score_bound = (NORM_SLACK * HEAD_DIM * Q_SCALE * jnp.max(jnp.abs(P["a_q_norm"][j]))
                           * jnp.max(jnp.abs(P["a_k_norm"][j])))
            mixed = _diff_attn(q[:, 0], k[:, 0], v[:, 0], P["a_lambda"][j].astype(F32), subln, lam_init,
                               score_bound)
            w_out = P["a_w_out"][j]
        else:
            pairs = B_HEADS // 2
            width = B_HEADS * HEAD_DIM
            w_in = P["b_w_in"][j]
            outs, lses = [], []
            for gi, (window, dil) in enumerate(B_GROUPS):
                cols = [w_in[:, s * len(B_GROUPS) * width + gi * width:][:, :width] for s in range(3)]
                if gi == 0:
                    cols.append(w_in[:, 3 * len(B_GROUPS) * width:])
                qg = jnp.tile(_pair_gain(P["b_q_norm"][j][gi], Q_SCALE), (pairs, 1))
                kg = jnp.tile(_pair_gain(P["b_k_norm"][j][gi]), (pairs, 1))
                q, k, v, *rest = _inproj(x, row("norm_mix"), jnp.concatenate(cols, axis=1).astype(BF16),
                                         ones, rope, qg, kg, xg, pairs, dil=dil, with_xq=(gi == 0))
                if gi == 0:
                    xq = rest[0]
                o, l = _band_attn(q, k, v, (window // 2) // dil)
                outs.append(o)
                lses.append(l)
            mixed = (outs, lses)
            w_out = P["b_w_out"][j]
        x = _outproj(mixed, xq, mk, mv, x, w_out.astype(BF16))
        x = _ffn(x, row("norm_ffn"), P["w_up"][i].astype(BF16), P["conv_w"][i].astype(F32),
                 P["conv_b"][i][None, :].astype(F32), P["w_down"][i].astype(BF16))
    return x


def kernel(x_prompt, x_sample, mem_prompt, mem_sample, norm_mix, norm_mem, w_mem_kv, xq_norm, xk_norm, a_w_in, a_w_out, a_q_norm, a_k_norm, a_lambda, a_subln, b_w_in, b_w_out, b_q_norm, b_k_norm, norm_ffn, w_up, conv_w, conv_b, w_down):
    P = dict(norm_mix=norm_mix, norm_mem=norm_mem, w_mem_kv=w_mem_kv, xq_norm=xq_norm, xk_norm=xk_norm,
             a_w_in=a_w_in, a_w_out=a_w_out, a_q_norm=a_q_norm, a_k_norm=a_k_norm, a_lambda=a_lambda,
             a_subln=a_subln, b_w_in=b_w_in, b_w_out=b_w_out, b_q_norm=b_q_norm, b_k_norm=b_k_norm,
             norm_ffn=norm_ffn, w_up=w_up, conv_w=conv_w, conv_b=conv_b, w_down=w_down)
    return _trunk(x_prompt, mem_prompt, P), _trunk(x_sample, mem_sample, P)
```

```python
import functools
import math

import jax
import jax.numpy as jnp
from jax import lax
from jax.experimental import pallas as pl
from jax.experimental.pallas import tpu as pltpu

F32 = jnp.float32
BF16 = jnp.bfloat16

LANES = 128
HEAD_DIM = 64
ROT_HALF = 8
A_HEADS = 8
B_HEADS = 8
B_GROUPS = ((128, 1), (512, 4), (2048, 16))
X_HEADS = 4
D_FF = 2816
ROPE_THETA = 500000.0
EPS = 1e-6
NEG_INF = -1e30
LOG2E = math.log2(math.e)
Q_SCALE = LOG2E / math.sqrt(HEAD_DIM)

V7X_VMEM_BYTES = 64 * 1024 * 1024
VMEM_LIMIT = V7X_VMEM_BYTES * 7 // 8

ROW_TILE = 512
INPROJ_ROW_TILE = 1024
OUTPROJ_ROW_TILE = 1024
FFN_ROW_TILE = 1024
Q_TILE_A = 2048
Q_SUB_A = 256
KEY_CHUNK = 512
SAFE_EXP2_RANGE = 50.0
NORM_SLACK = 1.02
Q_TILE_B = 128
MXU_COLS = 256


def _params(*sem):
    return pltpu.CompilerParams(dimension_semantics=sem, vmem_limit_bytes=VMEM_LIMIT)


def _full(shape):
    return pl.BlockSpec(shape, lambda *_: (0,) * len(shape))


def _resident(shape):
    return pl.BlockSpec(shape, lambda *_: (0,) * len(shape), pipeline_mode=pl.Buffered(1))


def _rms_rows(x, g):
    ms = jnp.mean(x * x, axis=-1, keepdims=True)
    return x * lax.rsqrt(ms + EPS) * g


def _lane_tiles(x):
    return [x[:, t * LANES:(t + 1) * LANES] for t in range(x.shape[1] // LANES)]


def _scale_rows(y, ssq, gain):
    return y * lax.rsqrt(ssq * (1.0 / HEAD_DIM) + EPS) * gain


def _pair_norm(y, group_ones, gain):
    ssq = jnp.dot((y * y).astype(BF16), group_ones, preferred_element_type=F32)
    return _scale_rows(y, ssq, gain)


def _rope(y, cos, sin_lo, sin_hi):
    return (y * cos + pltpu.roll(y, ROT_HALF, axis=1) * sin_lo
            + pltpu.roll(y, LANES - ROT_HALF, axis=1) * sin_hi)


def _split_heads(q):
    lane = lax.broadcasted_iota(jnp.int32, q.shape, 1)
    zero = jnp.zeros_like(q)
    return jnp.where(lane < HEAD_DIM, q, zero), jnp.where(lane >= HEAD_DIM, q, zero)


def _merge_heads(o_lo, o_hi):
    lane = lax.broadcasted_iota(jnp.int32, o_lo.shape, 1)
    return jnp.where(lane < HEAD_DIM, o_lo, o_hi)


def _scores(q, k):
    return lax.dot_general(q, k, (((1,), (1,)), ((), ())), preferred_element_type=F32)


def _inproj_kernel(nq, nk, nv, nx, x_ref, g_ref, w_ref, ones_ref, cos_ref, slo_ref, shi_ref,
                   qg_ref, kg_ref, xg_ref, *out_and_scratch):
    q_ref, k_ref, v_ref = out_and_scratch[:3]
    dil, tl = q_ref.shape[1], q_ref.shape[3]
    if dil == 1:
        h_ref = out_and_scratch[-1]
        h_ref[...] = _rms_rows(x_ref[0], g_ref[...]).astype(BF16)
    else:
        stage_ref, h_ref = out_and_scratch[-2:]
        h = _rms_rows(x_ref[0], g_ref[...])
        for c in range(stage_ref.shape[0]):
            stage_ref[c] = h[:, c * LANES:(c + 1) * LANES]
        for r in range(dil):
            for c in range(stage_ref.shape[0]):
                h_ref[r * tl:(r + 1) * tl, c * LANES:(c + 1) * LANES] = (
                    stage_ref[c, pl.ds(r, tl, stride=dil), :].astype(BF16))

    def put(ref, p, y):
        for r in range(dil):
            ref[0, r, p] = y[r * tl:(r + 1) * tl].astype(BF16)

    ones = ones_ref[...]
    cos, slo, shi = cos_ref[...], slo_ref[...], shi_ref[...]
    pairs_per_chunk = MXU_COLS // LANES
    n_chunks = (nq + nk + nv + nx) // pairs_per_chunk

    def project(ch):
        return jnp.dot(h_ref[...], w_ref[:, ch * MXU_COLS:(ch + 1) * MXU_COLS],
                       preferred_element_type=F32)

    acc_next = project(0)
    for ch in range(n_chunks):
        acc = acc_next
        if ch + 1 < n_chunks:
            acc_next = project(ch + 1)
        ys = _lane_tiles(acc)
        first = ch * pairs_per_chunk
        if first < nq + nk or first >= nq + nk + nv:
            ssq = _lane_tiles(jnp.dot((acc * acc).astype(BF16), ones, preferred_element_type=F32))
        for half, y in enumerate(ys):
            p = first + half
            if p < nq + nk:
                ref, gain = (q_ref, qg_ref) if p < nq else (k_ref, kg_ref)
                p = p if p < nq else p - nq
                put(ref, p, _rope(_scale_rows(y, ssq[half], gain[p:p + 1, :]), cos, slo, shi))
            elif p < nq + nk + nv:
                put(v_ref, p - nq - nk, y)
            else:
                put(out_and_scratch[3], p - nq - nk - nv, _scale_rows(y, ssq[half], xg_ref[...]))


def _inproj(x, g, w, ones, rope, qg, kg, xg, nv, dil=1, with_xq=True):
    B, S, D = x.shape
    nq, nk, nx = qg.shape[0], kg.shape[0], (X_HEADS // 2 if with_xq else 0)
    tm = INPROJ_ROW_TILE
    tl = tm // dil
    rope = [t.reshape(S // tm, tl, dil, LANES).transpose(0, 2, 1, 3).reshape(S, LANES) for t in rope]
    tab = pl.BlockSpec((tm, LANES), lambda b, i: (i, 0))
    sizes = [nq, nk, nv] + ([nx] if with_xq else [])
    scratch = [pltpu.VMEM((tm, D), BF16)]
    if dil > 1:
        scratch.insert(0, pltpu.VMEM((D // LANES, tm, LANES), F32))
    return pl.pallas_call(
        functools.partial(_inproj_kernel, nq, nk, nv, nx),
        grid=(B, S // tm),
        in_specs=[pl.BlockSpec((1, tm, D), lambda b, i: (b, i, 0)), _full(g.shape), _full(w.shape),
                  _full(ones.shape), tab, tab, tab, _full(qg.shape), _full(kg.shape), _full(xg.shape)],
        out_specs=[pl.BlockSpec((1, dil, n, tl, LANES), lambda b, i: (b, 0, 0, i, 0)) for n in sizes],
        out_shape=[jax.ShapeDtypeStruct((B, dil, n, S // dil, LANES), BF16) for n in sizes],
        scratch_shapes=scratch,
        compiler_params=_params("parallel", "parallel"),
        name="inproj",
    )(x, g, w, ones, *rope, qg, kg, xg)


def _exp2_rows(s):
    e = jnp.exp2(s - jnp.max(s, axis=-1, keepdims=True))
    return e.astype(BF16), jnp.sum(e, axis=-1, keepdims=True)


def _diff_attn_kernel(lam_init, shifted, q_ref, k_ref, v_ref, lam_ref, sg_ref, o_ref, s_ref):
    seq = k_ref.shape[2]
    ring, _, rows, _ = s_ref.shape
    kc = KEY_CHUNK
    n_sub = q_ref.shape[2] // rows
    lp = lam_ref[...]
    lam = (jnp.exp(jnp.sum(lp[0:1] * lp[1:2], axis=-1, keepdims=True))
           - jnp.exp(jnp.sum(lp[2:3] * lp[3:4], axis=-1, keepdims=True)) + lam_init)
    wide = lambda col: jnp.broadcast_to(col, (rows, LANES))
    lag = 1 if shifted else 0
    row_max, coef = {}, {}
    for it in range(n_sub + lag + 1):
        sb_qk, sb_exp, sb_pv = it, it - lag, it - lag - 1
        do_qk, do_exp, do_pv = sb_qk < n_sub, 0 <= sb_exp < n_sub, 0 <= sb_pv
        if do_qk:
            qs = _split_heads(q_ref[0, 0, sb_qk * rows:(sb_qk + 1) * rows, :])
            m_wide = [None, None]
        if do_exp:
            l_wide = [None, None]
        acc = None
        for c0 in range(0, seq, kc):
            cols = slice(c0, c0 + kc)
            if do_qk:
                for h in range(2):
                    s = _scores(qs[h], k_ref[0, 0, cols, :])
                    if shifted:
                        s_ref[sb_qk % ring, h, :, cols] = s
                        m_wide[h] = functools.reduce(
                            jnp.maximum, _lane_tiles(s) + ([] if m_wide[h] is None else [m_wide[h]]))
                    else:
                        e = [jnp.exp2(t) for t in _lane_tiles(s)]
                        s_ref[sb_qk % ring, h, :, cols] = jnp.concatenate(e, axis=1)
                        l_wide[h] = functools.reduce(jnp.add, e + ([] if l_wide[h] is None else [l_wide[h]]))
            if shifted and do_exp:
                for h in range(2):
                    e = [jnp.exp2(t - row_max[sb_exp][h]) for t in _lane_tiles(s_ref[sb_exp % ring, h, :, cols])]
                    s_ref[sb_exp % ring, h, :, cols] = jnp.concatenate(e, axis=1)
                    l_wide[h] = functools.reduce(jnp.add, e + ([] if l_wide[h] is None else [l_wide[h]]))
            if do_pv:
                c1, c2 = coef[sb_pv]
                diff = [t1 * c1 - t2 * c2 for t1, t2 in zip(_lane_tiles(s_ref[sb_pv % ring, 0, :, cols]),
                                                           _lane_tiles(s_ref[sb_pv % ring, 1, :, cols]))]
                pv = jnp.dot(jnp.concatenate(diff, axis=1).astype(BF16), v_ref[0, 0, cols, :],
                             preferred_element_type=F32)
                acc = pv if acc is None else acc + pv
        if shifted and do_qk:
            row_max[sb_qk] = [wide(jnp.max(m, axis=-1, keepdims=True)) for m in m_wide]
        if do_exp:
            l1, l2 = [jnp.sum(l, axis=-1, keepdims=True) for l in l_wide]
            coef[sb_exp] = (wide(1.0 / l1), wide(lam / l2))
        if do_pv:
            o_ref[0, sb_pv * rows:(sb_pv + 1) * rows, :] = _rms_rows(acc, sg_ref[...]).astype(BF16)


def _diff_attn_call(shifted, lam_init, q, k, v, lam_p, subln):
    B, H, S, _ = q.shape
    tq = Q_TILE_A
    kv = pl.BlockSpec((1, 1, S, LANES), lambda b, h, i: (b, h, 0, 0))
    stages_in_flight = 3 if shifted else 2
    return pl.pallas_call(
        functools.partial(_diff_attn_kernel, lam_init, shifted),
        grid=(B, H, S // tq),
        in_specs=[pl.BlockSpec((1, 1, tq, LANES), lambda b, h, i: (b, h, i, 0)), kv, kv,
                  _full(lam_p.shape), _full(subln.shape)],
        out_specs=pl.BlockSpec((1, tq, LANES), lambda b, h, i: (b, i, h)),
        out_shape=jax.ShapeDtypeStruct((B, S, H * LANES), BF16),
        scratch_shapes=[pltpu.VMEM((stages_in_flight, 2, Q_SUB_A, S), F32)],
        compiler_params=_params("parallel", "parallel", "parallel"),
        name="diff_attn_shifted" if shifted else "diff_attn",
    )(q, k, v, lam_p, subln)


def _diff_attn(q, k, v, lam_p, subln, lam_init, score_bound):
    return lax.cond(score_bound <= SAFE_EXP2_RANGE,
                    functools.partial(_diff_attn_call, False, lam_init),
                    functools.partial(_diff_attn_call, True, lam_init),
                    q, k, v, lam_p, subln)


def _band_attn_kernel(radius, q_ref, k_ref, v_ref, o_ref, l_ref):
    _, res, pairs, rows, _ = q_ref.shape
    L = k_ref.shape[3]
    tq = min(Q_TILE_B, L)
    span = min(L, tq + 2 * radius)
    row0 = pl.program_id(2) * rows
    col = lax.broadcasted_iota(jnp.int32, (tq, span), 1)
    qrow = lax.broadcasted_iota(jnp.int32, (tq, span), 0)
    for r in range(res):
        for blk in range(rows // tq):
            q0 = row0 + blk * tq
            start = pl.multiple_of(jnp.clip(q0 - radius, 0, L - span), radius)
            valid = jnp.abs(col - qrow + (start - q0)) <= radius
            rs = slice(blk * tq, (blk + 1) * tq)
            scores = [[_scores(qh, k_ref[0, r, p, pl.ds(start, span), :])
                       for qh in _split_heads(q_ref[0, r, p, rs, :])] for p in range(pairs)]
            probs = []
            for p in range(pairs):
                per_head = []
                for s in scores[p]:
                    s = jnp.where(valid, s, NEG_INF)
                    m = jnp.max(s, axis=-1, keepdims=True)
                    e = jnp.exp2(s - m)
                    z = jnp.sum(e, axis=-1, keepdims=True)
                    per_head.append((e.astype(BF16), z, m + jnp.log2(z)))
                probs.append(per_head)
            for p in range(pairs):
                vw = v_ref[0, r, p, pl.ds(start, span), :]
                outs = [jnp.dot(e, vw, preferred_element_type=F32) / z for e, z, _ in probs[p]]
                lses = [jnp.broadcast_to(lse, (tq, LANES)) for _, _, lse in probs[p]]
                o_ref[0, r, p, rs, :] = _merge_heads(*outs).astype(BF16)
                l_ref[0, r, p, rs, :] = _merge_heads(*lses)


def _band_attn(q, k, v, radius):
    B, dil, pairs, L, _ = q.shape
    rows = min(L, ROW_TILE)
    res = ROW_TILE // rows
    kv = pl.BlockSpec((1, res, pairs, L, LANES), lambda b, r, i: (b, r, 0, 0, 0))
    tile = pl.BlockSpec((1, res, pairs, rows, LANES), lambda b, r, i: (b, r, 0, i, 0))
    return pl.pallas_call(
        functools.partial(_band_attn_kernel, radius),
        grid=(B, dil // res, L // rows),
        in_specs=[tile, kv, kv],
        out_specs=[tile, tile],
        out_shape=[jax.ShapeDtypeStruct(q.shape, BF16), jax.ShapeDtypeStruct(q.shape, F32)],
        compiler_params=_params("parallel", "parallel", "parallel"),
        name="band_attn",
    )(q, k, v)


def _group_mix_into(o_refs, l_refs, stage_ref, dst_ref):
    def token_order(ref, p, slot):
        dil, tl = ref.shape[1], ref.shape[3]
        if dil == 1:
            return ref[0, 0, p].astype(F32)
        for r in range(dil):
            stage_ref[slot, pl.ds(r, tl, stride=dil), :] = ref[0, r, p].astype(F32)
        return stage_ref[slot]

    for p in range(o_refs[0].shape[2]):
        ls = [token_order(l, p, 2 * g) for g, l in enumerate(l_refs)]
        os_ = [token_order(o, p, 2 * g + 1) for g, o in enumerate(o_refs)]
        m = functools.reduce(jnp.maximum, ls)
        ws = [jnp.exp2(l - m) for l in ls]
        num = functools.reduce(jnp.add, [w * o for w, o in zip(ws, os_)])
        dst_ref[:, p * LANES:(p + 1) * LANES] = (num / functools.reduce(jnp.add, ws)).astype(BF16)


def _mem_kv_kernel(m_ref, g_ref, w_ref, ones_ref, kg_ref, k_ref, v_ref):
    h = _rms_rows(m_ref[0], g_ref[...]).astype(BF16)
    kv = jnp.dot(h, w_ref[...], preferred_element_type=F32)
    half = kv.shape[1] // 2
    for p in range(half // LANES):
        y = _pair_norm(kv[:, p * LANES:(p + 1) * LANES], ones_ref[...], kg_ref[...])
        k_ref[0, :, p * LANES:(p + 1) * LANES] = y.astype(BF16)
    v_ref[0] = kv[:, half:].astype(BF16)


def _mem_kv(mem, g, w, ones, kg):
    B, M, D = mem.shape
    width = w.shape[1] // 2
    blk = pl.BlockSpec((1, M, width), lambda b: (b, 0, 0))
    return pl.pallas_call(
        _mem_kv_kernel,
        grid=(B,),
        in_specs=[pl.BlockSpec((1, M, D), lambda b: (b, 0, 0)), _full(g.shape), _full(w.shape),
                  _full(ones.shape), _full(kg.shape)],
        out_specs=[blk, blk],
        out_shape=[jax.ShapeDtypeStruct((B, M, width), BF16)] * 2,
        compiler_params=_params("parallel"),
        name="mem_kv",
    )(mem, g, w, ones, kg)


def _cross_attn_into(q_ref, k_ref, v_ref, dst_ref, col0):
    pairs = range(q_ref.shape[2])
    cols = [slice(p * LANES, (p + 1) * LANES) for p in pairs]
    scores = [[_scores(qh, k_ref[0, :, cols[p]]) for qh in _split_heads(q_ref[0, 0, p])] for p in pairs]
    probs = [[_exp2_rows(s) for s in scores[p]] for p in pairs]
    for p in pairs:
        v = v_ref[0, :, cols[p]]
        outs = [jnp.dot(e, v, preferred_element_type=F32) / l for e, l in probs[p]]
        dst_ref[:, col0 + p * LANES:col0 + (p + 1) * LANES] = _merge_heads(*outs).astype(BF16)


def _outproj_kernel(n_groups, *refs):
    if n_groups:
        o_refs, l_refs = refs[:n_groups], refs[n_groups:2 * n_groups]
        xq_ref, mk_ref, mv_ref, x_ref, w_ref, out_ref, lhs_ref, stage_ref = refs[2 * n_groups:]
        mix_width = o_refs[0].shape[2] * LANES
        _group_mix_into(o_refs, l_refs, stage_ref, lhs_ref)
        _cross_attn_into(xq_ref, mk_ref, mv_ref, lhs_ref, mix_width)
        out_ref[0] = x_ref[0] + jnp.dot(lhs_ref[...], w_ref[...], preferred_element_type=F32)
    else:
        mix_ref, xq_ref, mk_ref, mv_ref, x_ref, w_ref, out_ref, lhs_ref = refs
        mix_width = mix_ref.shape[2]
        _cross_attn_into(xq_ref, mk_ref, mv_ref, lhs_ref, 0)
        out_ref[0] = (x_ref[0]
                      + jnp.dot(mix_ref[0], w_ref[:mix_width, :], preferred_element_type=F32)
                      + jnp.dot(lhs_ref[...], w_ref[mix_width:, :], preferred_element_type=F32))


def _outproj(mixed, xq, mk, mv, x, w):
    B, S, D = x.shape
    M, cross_width = mk.shape[1:]
    tm = OUTPROJ_ROW_TILE
    row = lambda c: pl.BlockSpec((1, tm, c), lambda b, i: (b, i, 0))
    kv = pl.BlockSpec((1, M, cross_width), lambda b, i: (b, 0, 0))
    common = [pl.BlockSpec((1, 1, xq.shape[2], tm, LANES), lambda b, i: (b, 0, 0, i, 0)), kv, kv,
              row(D), _full(w.shape)]
    if isinstance(mixed, tuple):
        groups = [*mixed[0], *mixed[1]]
        pairs = groups[0].shape[2]
        group_spec = lambda t: pl.BlockSpec((1, t.shape[1], pairs, tm // t.shape[1], LANES),
                                            lambda b, i: (b, 0, 0, i, 0))
        in_specs = [group_spec(t) for t in groups] + common
        args = groups
        scratch = [pltpu.VMEM((tm, pairs * LANES + cross_width), BF16),
                   pltpu.VMEM((len(groups), tm, LANES), F32)]
        n_groups = len(mixed[0])
    else:
        in_specs = [row(mixed.shape[2])] + common
        args = [mixed]
        scratch = [pltpu.VMEM((tm, cross_width), BF16)]
        n_groups = 0
    return pl.pallas_call(
        functools.partial(_outproj_kernel, n_groups),
        grid=(B, S // tm),
        in_specs=in_specs,
        out_specs=row(D),
        out_shape=jax.ShapeDtypeStruct((B, S, D), F32),
        scratch_shapes=scratch,
        compiler_params=_params("parallel", "parallel"),
        name="outproj",
    )(*args, xq, mk, mv, x, w)


FF_CHUNK = 256


HALO_ROWS = 16


def _ffn_kernel(x_ref, prev_ref, next_ref, g_ref, wu_ref, cw_ref, cb_ref, wd_ref, o_ref,
                h_ref, act_ref):
    tm = x_ref.shape[1]
    i, n = pl.program_id(1), pl.num_programs(1)
    g = g_ref[...]
    h_ref[0:HALO_ROWS] = (_rms_rows(prev_ref[0], g) * (i > 0).astype(F32)).astype(BF16)
    h_ref[HALO_ROWS:HALO_ROWS + tm] = _rms_rows(x_ref[0], g).astype(BF16)
    h_ref[HALO_ROWS + tm:] = (_rms_rows(next_ref[0], g) * (i < n - 1).astype(F32)).astype(BF16)
    main = slice(HALO_ROWS, HALO_ROWS + tm)

    def project(c0):
        return [jnp.dot(h_ref[...], wu_ref[:, col:col + FF_CHUNK], preferred_element_type=F32)
                for col in (c0, D_FF + c0)]

    def conv(u, col):
        cols = slice(col, col + FF_CHUNK)
        before = pltpu.roll(u, 1, axis=0)[main]
        after = pltpu.roll(u, u.shape[0] - 1, axis=0)[main]
        return (before * cw_ref[0:1, cols] + u[main] * cw_ref[1:2, cols] + after * cw_ref[2:3, cols]
                + cb_ref[:, cols])

    starts = list(range(0, D_FF, FF_CHUNK))
    u_next = project(starts[0])
    for idx, c0 in enumerate(starts):
        u_a, u_b = u_next
        if idx + 1 < len(starts):
            u_next = project(starts[idx + 1])
        a, b = conv(u_a, c0), conv(u_b, D_FF + c0)
        act_ref[:, c0:c0 + FF_CHUNK] = (a / (1.0 + jnp.exp(-a)) * b).astype(BF16)
    o_ref[0] = x_ref[0] + jnp.dot(act_ref[...], wd_ref[...], preferred_element_type=F32)


def _ffn(x, g, w_up, conv_w, conv_b, w_down):
    B, S, D = x.shape
    tm = FFN_ROW_TILE
    halo_per_tile = tm // HALO_ROWS
    last_halo = S // HALO_ROWS - 1
    tile = pl.BlockSpec((1, tm, D), lambda b, i: (b, i, 0))
    return pl.pallas_call(
        _ffn_kernel,
        grid=(B, S // tm),
        in_specs=[tile,
                  pl.BlockSpec((1, HALO_ROWS, D), lambda b, i: (b, jnp.maximum(i * halo_per_tile - 1, 0), 0)),
                  pl.BlockSpec((1, HALO_ROWS, D),
                               lambda b, i: (b, jnp.minimum((i + 1) * halo_per_tile, last_halo), 0)),
                  _full(g.shape), _resident(w_up.shape), _full(conv_w.shape), _full(conv_b.shape),
                  _resident(w_down.shape)],
        out_specs=tile,
        out_shape=jax.ShapeDtypeStruct((B, S, D), F32),
        scratch_shapes=[pltpu.VMEM((tm + 2 * HALO_ROWS, D), BF16), pltpu.VMEM((tm, D_FF), BF16)],
        compiler_params=_params("parallel", "parallel"),
        name="ffn",
    )(x, x, x, g, w_up, conv_w, conv_b, w_down)


def _rope_tables(S):
    inv = ROPE_THETA ** (-(jnp.arange(ROT_HALF, dtype=F32) * 2.0 / (2 * ROT_HALF)))
    ang = jnp.arange(S, dtype=F32)[:, None] * inv[None, :]
    cos, sin = jnp.cos(ang), jnp.sin(ang)
    rest = HEAD_DIM - 2 * ROT_HALF
    one, zero = jnp.ones((S, rest), F32), jnp.zeros((S, rest), F32)
    zhalf = jnp.zeros((S, ROT_HALF), F32)
    per_head = (jnp.concatenate([cos, cos, one], axis=1),
                jnp.concatenate([zhalf, sin, zero], axis=1),
                jnp.concatenate([-sin, zhalf, zero], axis=1))
    return tuple(jnp.tile(t, (1, LANES // HEAD_DIM)) for t in per_head)


def _pair_gain(g, scale=1.0):
    g = jnp.atleast_2d(g.astype(F32) * scale)
    return jnp.tile(g, (1, LANES // HEAD_DIM))


def _group_ones(width):
    head = jnp.arange(width) // HEAD_DIM
    return (head[:, None] == head[None, :]).astype(BF16)


def _a_weight(w):
    d = w.shape[0]
    n = A_HEADS * HEAD_DIM
    pair = lambda t: t.reshape(d, 2, A_HEADS, HEAD_DIM).transpose(0, 2, 1, 3).reshape(d, 2 * n)
    return jnp.concatenate([pair(w[:, :2 * n]), pair(w[:, 2 * n:4 * n]), w[:, 4 * n:]], axis=1).astype(BF16)


def _trunk(x, mem, P):
    S = x.shape[1]
    rope = _rope_tables(S)
    ones = _group_ones(MXU_COLS)
    n_layers = P["norm_mix"].shape[0]
    for i in range(n_layers):
        j = i // 2
        row = lambda name: P[name][i][None, :].astype(F32)
        mk, mv = _mem_kv(mem, row("norm_mem"), P["w_mem_kv"][i].astype(BF16), _group_ones(LANES),
                         _pair_gain(P["xk_norm"][i]))
        xg = _pair_gain(P["xq_norm"][i], Q_SCALE)
        if i % 2 == 0:
            qg = jnp.tile(_pair_gain(P["a_q_norm"][j], Q_SCALE), (A_HEADS, 1))
            kg = jnp.tile(_pair_gain(P["a_k_norm"][j]), (A_HEADS, 1))
            q, k, v, xq = _inproj(x, row("norm_mix"), _a_weight(P["a_w_in"][j]), ones, rope,
                                  qg, kg, xg, A_HEADS)
            lam_init = 0.8 - 0.6 * math.exp(-0.3 * i)
            subln = P["a_subln"][j][None, :].astype(F32) * (1.0 - lam_init)
            score_bound = (NORM_SLACK * HEAD_DIM * Q_SCALE * jnp.max(jnp.abs(P["a_q_norm"][j]))
                           * jnp.max(jnp.abs(P["a_k_norm"][j])))
            mixed = _diff_attn(q[:, 0], k[:, 0], v[:, 0], P["a_lambda"][j].astype(F32), subln, lam_init,
                               score_bound)
            w_out = P["a_w_out"][j]
        else:
            pairs = B_HEADS // 2
            width = B_HEADS * HEAD_DIM
            w_in = P["b_w_in"][j]
            outs, lses = [], []
            for gi, (window, dil) in enumerate(B_GROUPS):
                cols = [w_in[:, s * len(B_GROUPS) * width + gi * width:][:, :width] for s in range(3)]
                if gi == 0:
                    cols.append(w_in[:, 3 * len(B_GROUPS) * width:])
                qg = jnp.tile(_pair_gain(P["b_q_norm"][j][gi], Q_SCALE), (pairs, 1))
                kg = jnp.tile(_pair_gain(P["b_k_norm"][j][gi]), (pairs, 1))
                q, k, v, *rest = _inproj(x, row("norm_mix"), jnp.concatenate(cols, axis=1).astype(BF16),
                                         ones, rope, qg, kg, xg, pairs, dil=dil, with_xq=(gi == 0))
                if gi == 0:
                    xq = rest[0]
                o, l = _band_attn(q, k, v, (window // 2) // dil)
                outs.append(o)
                lses.append(l)
            mixed = (outs, lses)
            w_out = P["b_w_out"][j]
        x = _outproj(mixed, xq, mk, mv, x, w_out.astype(BF16))
        x = _ffn(x, row("norm_ffn"), P["w_up"][i].astype(BF16), P["conv_w"][i].astype(F32),
                 P["conv_b"][i][None, :].astype(F32), P["w_down"][i].astype(BF16))
    return x


def kernel(x_prompt, x_sample, mem_prompt, mem_sample, norm_mix, norm_mem, w_mem_kv, xq_norm, xk_norm, a_w_in, a_w_out, a_q_norm, a_k_norm, a_lambda, a_subln, b_w_in, b_w_out, b_q_norm, b_k_norm, norm_ffn, w_up, conv_w, conv_b, w_down):
    P = dict(norm_mix=norm_mix, norm_mem=norm_mem, w_mem_kv=w_mem_kv, xq_norm=xq_norm, xk_norm=xk_norm,
             a_w_in=a_w_in, a_w_out=a_w_out, a_q_norm=a_q_norm, a_k_norm=a_k_norm, a_lambda=a_lambda,
             a_subln=a_subln, b_w_in=b_w_in, b_w_out=b_w_out, b_q_norm=b_q_norm, b_k_norm=b_k_norm,
             norm_ffn=norm_ffn, w_up=w_up, conv_w=conv_w, conv_b=conv_b, w_down=w_down)
    return _trunk(x_prompt, mem_prompt, P), _trunk(x_sample, mem_sample, P)
```

```python
import functools
import math

import jax
import jax.numpy as jnp
from jax import lax
from jax.experimental import pallas as pl
from jax.experimental.pallas import tpu as pltpu

F32 = jnp.float32
BF16 = jnp.bfloat16

LANES = 128
HEAD_DIM = 64
ROT_HALF = 8
A_HEADS = 8
B_HEADS = 8
B_GROUPS = ((128, 1), (512, 4), (2048, 16))
X_HEADS = 4
D_FF = 2816
ROPE_THETA = 500000.0
EPS = 1e-6
NEG_INF = -1e30
LOG2E = math.log2(math.e)
Q_SCALE = LOG2E / math.sqrt(HEAD_DIM)

V7X_VMEM_BYTES = 64 * 1024 * 1024
VMEM_LIMIT = V7X_VMEM_BYTES * 7 // 8

ROW_TILE = 512
INPROJ_ROW_TILE = 1024
OUTPROJ_ROW_TILE = 1024
FFN_ROW_TILE = 1024
Q_TILE_A = 2048
Q_SUB_A = 256
KEY_CHUNK = 512
SAFE_EXP2_RANGE = 50.0
NORM_SLACK = 1.02
Q_TILE_B = 128
MXU_COLS = 256


def _params(*sem):
    return pltpu.CompilerParams(dimension_semantics=sem, vmem_limit_bytes=VMEM_LIMIT)


def _full(shape):
    return pl.BlockSpec(shape, lambda *_: (0,) * len(shape))


def _resident(shape):
    return pl.BlockSpec(shape, lambda *_: (0,) * len(shape), pipeline_mode=pl.Buffered(1))


def _rms_rows(x, g):
    ms = jnp.mean(x * x, axis=-1, keepdims=True)
    return x * lax.rsqrt(ms + EPS) * g


def _lane_tiles(x):
    return [x[:, t * LANES:(t + 1) * LANES] for t in range(x.shape[1] // LANES)]


def _scale_rows(y, ssq, gain):
    return y * lax.rsqrt(ssq * (1.0 / HEAD_DIM) + EPS) * gain


def _pair_norm(y, group_ones, gain):
    ssq = jnp.dot((y * y).astype(BF16), group_ones, preferred_element_type=F32)
    return _scale_rows(y, ssq, gain)


def _rope(y, cos, sin_lo, sin_hi):
    return (y * cos + pltpu.roll(y, ROT_HALF, axis=1) * sin_lo
            + pltpu.roll(y, LANES - ROT_HALF, axis=1) * sin_hi)


def _split_heads(q):
    lane = lax.broadcasted_iota(jnp.int32, q.shape, 1)
    zero = jnp.zeros_like(q)
    return jnp.where(lane < HEAD_DIM, q, zero), jnp.where(lane >= HEAD_DIM, q, zero)


def _merge_heads(o_lo, o_hi):
    lane = lax.broadcasted_iota(jnp.int32, o_lo.shape, 1)
    return jnp.where(lane < HEAD_DIM, o_lo, o_hi)


def _scores(q, k):
    return lax.dot_general(q, k, (((1,), (1,)), ((), ())), preferred_element_type=F32)


def _inproj_kernel(nq, nk, nv, nx, x_ref, g_ref, w_ref, ones_ref, cos_ref, slo_ref, shi_ref,
                   qg_ref, kg_ref, xg_ref, *out_and_scratch):
    q_ref, k_ref, v_ref = out_and_scratch[:3]
    dil, tl = q_ref.shape[1], q_ref.shape[3]
    if dil == 1:
        h_ref = out_and_scratch[-1]
        h_ref[...] = _rms_rows(x_ref[0], g_ref[...]).astype(BF16)
    else:
        stage_ref, h_ref = out_and_scratch[-2:]
        h = _rms_rows(x_ref[0], g_ref[...])
        for c in range(stage_ref.shape[0]):
            stage_ref[c] = h[:, c * LANES:(c + 1) * LANES]
        for r in range(dil):
            for c in range(stage_ref.shape[0]):
                h_ref[r * tl:(r + 1) * tl, c * LANES:(c + 1) * LANES] = (
                    stage_ref[c, pl.ds(r, tl, stride=dil), :].astype(BF16))

    def put(ref, p, y):
        for r in range(dil):
            ref[0, r, p] = y[r * tl:(r + 1) * tl].astype(BF16)

    ones = ones_ref[...]
    cos, slo, shi = cos_ref[...], slo_ref[...], shi_ref[...]
    pairs_per_chunk = MXU_COLS // LANES
    n_chunks = (nq + nk + nv + nx) // pairs_per_chunk

    def project(ch):
        return jnp.dot(h_ref[...], w_ref[:, ch * MXU_COLS:(ch + 1) * MXU_COLS],
                       preferred_element_type=F32)

    acc_next = project(0)
    for ch in range(n_chunks):
        acc = acc_next
        if ch + 1 < n_chunks:
            acc_next = project(ch + 1)
        ys = _lane_tiles(acc)
        first = ch * pairs_per_chunk
        if first < nq + nk or first >= nq + nk + nv:
            ssq = _lane_tiles(jnp.dot((acc * acc).astype(BF16), ones, preferred_element_type=F32))
        for half, y in enumerate(ys):
            p = first + half
            if p < nq + nk:
                ref, gain = (q_ref, qg_ref) if p < nq else (k_ref, kg_ref)
                p = p if p < nq else p - nq
                put(ref, p, _rope(_scale_rows(y, ssq[half], gain[p:p + 1, :]), cos, slo, shi))
            elif p < nq + nk + nv:
                put(v_ref, p - nq - nk, y)
            else:
                put(out_and_scratch[3], p - nq - nk - nv, _scale_rows(y, ssq[half], xg_ref[...]))


def _inproj(x, g, w, ones, rope, qg, kg, xg, nv, dil=1, with_xq=True):
    B, S, D = x.shape
    nq, nk, nx = qg.shape[0], kg.shape[0], (X_HEADS // 2 if with_xq else 0)
    tm = INPROJ_ROW_TILE
    tl = tm // dil
    rope = [t.reshape(S // tm, tl, dil, LANES).transpose(0, 2, 1, 3).reshape(S, LANES) for t in rope]
    tab = pl.BlockSpec((tm, LANES), lambda b, i: (i, 0))
    sizes = [nq, nk, nv] + ([nx] if with_xq else [])
    scratch = [pltpu.VMEM((tm, D), BF16)]
    if dil > 1:
        scratch.insert(0, pltpu.VMEM((D // LANES, tm, LANES), F32))
    return pl.pallas_call(
        functools.partial(_inproj_kernel, nq, nk, nv, nx),
        grid=(B, S // tm),
        in_specs=[pl.BlockSpec((1, tm, D), lambda b, i: (b, i, 0)), _full(g.shape), _full(w.shape),
                  _full(ones.shape), tab, tab, tab, _full(qg.shape), _full(kg.shape), _full(xg.shape)],
        out_specs=[pl.BlockSpec((1, dil, n, tl, LANES), lambda b, i: (b, 0, 0, i, 0)) for n in sizes],
        out_shape=[jax.ShapeDtypeStruct((B, dil, n, S // dil, LANES), BF16) for n in sizes],
        scratch_shapes=scratch,
        compiler_params=_params("parallel", "parallel"),
        name="inproj",
    )(x, g, w, ones, *rope, qg, kg, xg)


def _exp2_rows(s):
    e = jnp.exp2(s - jnp.max(s, axis=-1, keepdims=True))
    return e.astype(BF16), jnp.sum(e, axis=-1, keepdims=True)


def _diff_attn_kernel(lam_init, shifted, q_ref, k_ref, v_ref, lam_ref, sg_ref, o_ref, s_ref):
    seq = k_ref.shape[2]
    ring, _, rows, _ = s_ref.shape
    kc = KEY_CHUNK
    n_sub = q_ref.shape[2] // rows
    lp = lam_ref[...]
    lam = (jnp.exp(jnp.sum(lp[0:1] * lp[1:2], axis=-1, keepdims=True))
           - jnp.exp(jnp.sum(lp[2:3] * lp[3:4], axis=-1, keepdims=True)) + lam_init)
    wide = lambda col: jnp.broadcast_to(col, (rows, LANES))
    lag = 1 if shifted else 0
    row_max, coef = {}, {}
    for it in range(n_sub + lag + 1):
        sb_qk, sb_exp, sb_pv = it, it - lag, it - lag - 1
        do_qk, do_exp, do_pv = sb_qk < n_sub, 0 <= sb_exp < n_sub, 0 <= sb_pv
        if do_qk:
            qs = _split_heads(q_ref[0, 0, sb_qk * rows:(sb_qk + 1) * rows, :])
            m_wide = [None, None]
        if do_exp:
            l_wide = [None, None]
        acc = None
        for c0 in range(0, seq, kc):
            cols = slice(c0, c0 + kc)
            if do_qk:
                for h in range(2):
                    s = _scores(qs[h], k_ref[0, 0, cols, :])
                    if shifted:
                        s_ref[sb_qk % ring, h, :, cols] = s
                        m_wide[h] = functools.reduce(
                            jnp.maximum, _lane_tiles(s) + ([] if m_wide[h] is None else [m_wide[h]]))
                    else:
                        e = [jnp.exp2(t) for t in _lane_tiles(s)]
                        s_ref[sb_qk % ring, h, :, cols] = jnp.concatenate(e, axis=1)
                        l_wide[h] = functools.reduce(jnp.add, e + ([] if l_wide[h] is None else [l_wide[h]]))
            if shifted and do_exp:
                for h in range(2):
                    e = [jnp.exp2(t - row_max[sb_exp][h]) for t in _lane_tiles(s_ref[sb_exp % ring, h, :, cols])]
                    s_ref[sb_exp % ring, h, :, cols] = jnp.concatenate(e, axis=1)
                    l_wide[h] = functools.reduce(jnp.add, e + ([] if l_wide[h] is None else [l_wide[h]]))
            if do_pv:
                c1, c2 = coef[sb_pv]
                diff = [t1 * c1 - t2 * c2 for t1, t2 in zip(_lane_tiles(s_ref[sb_pv % ring, 0, :, cols]),
                                                           _lane_tiles(s_ref[sb_pv % ring, 1, :, cols]))]
                pv = jnp.dot(jnp.concatenate(diff, axis=1).astype(BF16), v_ref[0, 0, cols, :],
                             preferred_element_type=F32)
                acc = pv if acc is None else acc + pv
        if shifted and do_qk:
            row_max[sb_qk] = [wide(jnp.max(m, axis=-1, keepdims=True)) for m in m_wide]
        if do_exp:
            l1, l2 = [jnp.sum(l, axis=-1, keepdims=True) for l in l_wide]
            coef[sb_exp] = (wide(1.0 / l1), wide(lam / l2))
        if do_pv:
            o_ref[0, sb_pv * rows:(sb_pv + 1) * rows, :] = _rms_rows(acc, sg_ref[...]).astype(BF16)


def _diff_attn_call(shifted, lam_init, q, k, v, lam_p, subln):
    B, H, S, _ = q.shape
    tq = Q_TILE_A
    kv = pl.BlockSpec((1, 1, S, LANES), lambda b, h, i: (b, h, 0, 0))
    stages_in_flight = 3 if shifted else 2
    return pl.pallas_call(
        functools.partial(_diff_attn_kernel, lam_init, shifted),
        grid=(B, H, S // tq),
        in_specs=[pl.BlockSpec((1, 1, tq, LANES), lambda b, h, i: (b, h, i, 0)), kv, kv,
                  _full(lam_p.shape), _full(subln.shape)],
        out_specs=pl.BlockSpec((1, tq, LANES), lambda b, h, i: (b, i, h)),
        out_shape=jax.ShapeDtypeStruct((B, S, H * LANES), BF16),
        scratch_shapes=[pltpu.VMEM((stages_in_flight, 2, Q_SUB_A, S), F32)],
        compiler_params=_params("parallel", "parallel", "parallel"),
        name="diff_attn_shifted" if shifted else "diff_attn",
    )(q, k, v, lam_p, subln)


def _diff_attn(q, k, v, lam_p, subln, lam_init, score_bound):
    return lax.cond(score_bound <= SAFE_EXP2_RANGE,
                    functools.partial(_diff_attn_call, False, lam_init),
                    functools.partial(_diff_attn_call, True, lam_init),
                    q, k, v, lam_p, subln)


def _band_attn_kernel(radius, shifted, q_ref, k_ref, v_ref, o_ref, l_ref):
    _, res, pairs, rows, _ = q_ref.shape
    L = k_ref.shape[3]
    tq = min(Q_TILE_B, L)
    span = min(L, tq + 2 * radius)
    row0 = pl.program_id(2) * rows
    col = lax.broadcasted_iota(jnp.int32, (tq, span), 1)
    qrow = lax.broadcasted_iota(jnp.int32, (tq, span), 0)
    for r in range(res):
        for blk in range(rows // tq):
            q0 = row0 + blk * tq
            start = pl.multiple_of(jnp.clip(q0 - radius, 0, L - span), radius)
            valid = jnp.abs(col - qrow + (start - q0)) <= radius
            rs = slice(blk * tq, (blk + 1) * tq)
            scores = [[_scores(qh, k_ref[0, r, p, pl.ds(start, span), :])
                       for qh in _split_heads(q_ref[0, r, p, rs, :])] for p in range(pairs)]
            probs = []
            for p in range(pairs):
                per_head = []
                for s in scores[p]:
                    s = jnp.where(valid, s, NEG_INF)
                    m = jnp.max(s, axis=-1, keepdims=True) if shifted else 0.0
                    e = jnp.exp2(s - m) if shifted else jnp.exp2(s)
                    z = jnp.sum(e, axis=-1, keepdims=True)
                    per_head.append((e.astype(BF16), z, m + jnp.log2(z)))
                probs.append(per_head)
            for p in range(pairs):
                vw = v_ref[0, r, p, pl.ds(start, span), :]
                outs = [jnp.dot(e, vw, preferred_element_type=F32) / z for e, z, _ in probs[p]]
                lses = [jnp.broadcast_to(lse, (tq, LANES)) for _, _, lse in probs[p]]
                o_ref[0, r, p, rs, :] = _merge_heads(*outs).astype(BF16)
                l_ref[0, r, p, rs, :] = _merge_heads(*lses)


def _band_attn_call(radius, shifted, q, k, v):
    B, dil, pairs, L, _ = q.shape
    rows = min(L, ROW_TILE)
    res = ROW_TILE // rows
    kv = pl.BlockSpec((1, res, pairs, L, LANES), lambda b, r, i: (b, r, 0, 0, 0))
    tile = pl.BlockSpec((1, res, pairs, rows, LANES), lambda b, r, i: (b, r, 0, i, 0))
    return pl.pallas_call(
        functools.partial(_band_attn_kernel, radius, shifted),
        grid=(B, dil // res, L // rows),
        in_specs=[tile, kv, kv],
        out_specs=[tile, tile],
        out_shape=[jax.ShapeDtypeStruct(q.shape, BF16), jax.ShapeDtypeStruct(q.shape, F32)],
        compiler_params=_params("parallel", "parallel", "parallel"),
        name="band_attn_shifted" if shifted else "band_attn",
    )(q, k, v)


def _band_attn(q, k, v, radius, score_bound):
    return lax.cond(score_bound <= SAFE_EXP2_RANGE,
                    functools.partial(_band_attn_call, radius, False),
                    functools.partial(_band_attn_call, radius, True), q, k, v)


def _group_mix_into(o_refs, l_refs, stage_ref, dst_ref):
    def token_order(ref, p, slot):
        dil, tl = ref.shape[1], ref.shape[3]
        if dil == 1:
            return ref[0, 0, p].astype(F32)
        for r in range(dil):
            stage_ref[slot, pl.ds(r, tl, stride=dil), :] = ref[0, r, p].astype(F32)
        return stage_ref[slot]

    for p in range(o_refs[0].shape[2]):
        ls = [token_order(l, p, 2 * g) for g, l in enumerate(l_refs)]
        os_ = [token_order(o, p, 2 * g + 1) for g, o in enumerate(o_refs)]
        m = functools.reduce(jnp.maximum, ls)
        ws = [jnp.exp2(l - m) for l in ls]
        num = functools.reduce(jnp.add, [w * o for w, o in zip(ws, os_)])
        dst_ref[:, p * LANES:(p + 1) * LANES] = (num / functools.reduce(jnp.add, ws)).astype(BF16)


def _mem_kv_kernel(m_ref, g_ref, w_ref, ones_ref, kg_ref, k_ref, v_ref):
    h = _rms_rows(m_ref[0], g_ref[...]).astype(BF16)
    kv = jnp.dot(h, w_ref[...], preferred_element_type=F32)
    half = kv.shape[1] // 2
    for p in range(half // LANES):
        y = _pair_norm(kv[:, p * LANES:(p + 1) * LANES], ones_ref[...], kg_ref[...])
        k_ref[0, :, p * LANES:(p + 1) * LANES] = y.astype(BF16)
    v_ref[0] = kv[:, half:].astype(BF16)


def _mem_kv(mem, g, w, ones, kg):
    B, M, D = mem.shape
    width = w.shape[1] // 2
    blk = pl.BlockSpec((1, M, width), lambda b: (b, 0, 0))
    return pl.pallas_call(
        _mem_kv_kernel,
        grid=(B,),
        in_specs=[pl.BlockSpec((1, M, D), lambda b: (b, 0, 0)), _full(g.shape), _full(w.shape),
                  _full(ones.shape), _full(kg.shape)],
        out_specs=[blk, blk],
        out_shape=[jax.ShapeDtypeStruct((B, M, width), BF16)] * 2,
        compiler_params=_params("parallel"),
        name="mem_kv",
    )(mem, g, w, ones, kg)


def _cross_attn_into(q_ref, k_ref, v_ref, dst_ref, col0):
    pairs = range(q_ref.shape[2])
    cols = [slice(p * LANES, (p + 1) * LANES) for p in pairs]
    scores = [[_scores(qh, k_ref[0, :, cols[p]]) for qh in _split_heads(q_ref[0, 0, p])] for p in pairs]
    probs = [[_exp2_rows(s) for s in scores[p]] for p in pairs]
    for p in pairs:
        v = v_ref[0, :, cols[p]]
        outs = [jnp.dot(e, v, preferred_element_type=F32) / l for e, l in probs[p]]
        dst_ref[:, col0 + p * LANES:col0 + (p + 1) * LANES] = _merge_heads(*outs).astype(BF16)


def _outproj_kernel(n_groups, *refs):
    if n_groups:
        o_refs, l_refs = refs[:n_groups], refs[n_groups:2 * n_groups]
        xq_ref, mk_ref, mv_ref, x_ref, w_ref, out_ref, lhs_ref, stage_ref = refs[2 * n_groups:]
        mix_width = o_refs[0].shape[2] * LANES
        _group_mix_into(o_refs, l_refs, stage_ref, lhs_ref)
        _cross_attn_into(xq_ref, mk_ref, mv_ref, lhs_ref, mix_width)
        out_ref[0] = x_ref[0] + jnp.dot(lhs_ref[...], w_ref[...], preferred_element_type=F32)
    else:
        mix_ref, xq_ref, mk_ref, mv_ref, x_ref, w_ref, out_ref, lhs_ref = refs
        mix_width = mix_ref.shape[2]
        _cross_attn_into(xq_ref, mk_ref, mv_ref, lhs_ref, 0)
        out_ref[0] = (x_ref[0]
                      + jnp.dot(mix_ref[0], w_ref[:mix_width, :], preferred_element_type=F32)
                      + jnp.dot(lhs_ref[...], w_ref[mix_width:, :], preferred_element_type=F32))


def _outproj(mixed, xq, mk, mv, x, w):
    B, S, D = x.shape
    M, cross_width = mk.shape[1:]
    tm = OUTPROJ_ROW_TILE
    row = lambda c: pl.BlockSpec((1, tm, c), lambda b, i: (b, i, 0))
    kv = pl.BlockSpec((1, M, cross_width), lambda b, i: (b, 0, 0))
    common = [pl.BlockSpec((1, 1, xq.shape[2], tm, LANES), lambda b, i: (b, 0, 0, i, 0)), kv, kv,
              row(D), _full(w.shape)]
    if isinstance(mixed, tuple):
        groups = [*mixed[0], *mixed[1]]
        pairs = groups[0].shape[2]
        group_spec = lambda t: pl.BlockSpec((1, t.shape[1], pairs, tm // t.shape[1], LANES),
                                            lambda b, i: (b, 0, 0, i, 0))
        in_specs = [group_spec(t) for t in groups] + common
        args = groups
        scratch = [pltpu.VMEM((tm, pairs * LANES + cross_width), BF16),
                   pltpu.VMEM((len(groups), tm, LANES), F32)]
        n_groups = len(mixed[0])
    else:
        in_specs = [row(mixed.shape[2])] + common
        args = [mixed]
        scratch = [pltpu.VMEM((tm, cross_width), BF16)]
        n_groups = 0
    return pl.pallas_call(
        functools.partial(_outproj_kernel, n_groups),
        grid=(B, S // tm),
        in_specs=in_specs,
        out_specs=row(D),
        out_shape=jax.ShapeDtypeStruct((B, S, D), F32),
        scratch_shapes=scratch,
        compiler_params=_params("parallel", "parallel"),
        name="outproj",
    )(*args, xq, mk, mv, x, w)


FF_CHUNK = 256


HALO_ROWS = 16


def _ffn_kernel(x_ref, prev_ref, next_ref, g_ref, wu_ref, cw_ref, cb_ref, wd_ref, o_ref,
                h_ref, act_ref):
    tm = x_ref.shape[1]
    i, n = pl.program_id(1), pl.num_programs(1)
    g = g_ref[...]
    h_ref[0:HALO_ROWS] = (_rms_rows(prev_ref[0], g) * (i > 0).astype(F32)).astype(BF16)
    h_ref[HALO_ROWS:HALO_ROWS + tm] = _rms_rows(x_ref[0], g).astype(BF16)
    h_ref[HALO_ROWS + tm:] = (_rms_rows(next_ref[0], g) * (i < n - 1).astype(F32)).astype(BF16)
    main = slice(HALO_ROWS, HALO_ROWS + tm)

    def project(c0):
        return [jnp.dot(h_ref[...], wu_ref[:, col:col + FF_CHUNK], preferred_element_type=F32)
                for col in (c0, D_FF + c0)]

    def conv(u, col):
        cols = slice(col, col + FF_CHUNK)
        before = pltpu.roll(u, 1, axis=0)[main]
        after = pltpu.roll(u, u.shape[0] - 1, axis=0)[main]
        return (before * cw_ref[0:1, cols] + u[main] * cw_ref[1:2, cols] + after * cw_ref[2:3, cols]
                + cb_ref[:, cols])

    starts = list(range(0, D_FF, FF_CHUNK))
    u_next = project(starts[0])
    for idx, c0 in enumerate(starts):
        u_a, u_b = u_next
        if idx + 1 < len(starts):
            u_next = project(starts[idx + 1])
        a, b = conv(u_a, c0), conv(u_b, D_FF + c0)
        act_ref[:, c0:c0 + FF_CHUNK] = (a / (1.0 + jnp.exp(-a)) * b).astype(BF16)
    o_ref[0] = x_ref[0] + jnp.dot(act_ref[...], wd_ref[...], preferred_element_type=F32)


def _ffn(x, g, w_up, conv_w, conv_b, w_down):
    B, S, D = x.shape
    tm = FFN_ROW_TILE
    halo_per_tile = tm // HALO_ROWS
    last_halo = S // HALO_ROWS - 1
    tile = pl.BlockSpec((1, tm, D), lambda b, i: (b, i, 0))
    return pl.pallas_call(
        _ffn_kernel,
        grid=(B, S // tm),
        in_specs=[tile,
                  pl.BlockSpec((1, HALO_ROWS, D), lambda b, i: (b, jnp.maximum(i * halo_per_tile - 1, 0), 0)),
                  pl.BlockSpec((1, HALO_ROWS, D),
                               lambda b, i: (b, jnp.minimum((i + 1) * halo_per_tile, last_halo), 0)),
                  _full(g.shape), _resident(w_up.shape), _full(conv_w.shape), _full(conv_b.shape),
                  _resident(w_down.shape)],
        out_specs=tile,
        out_shape=jax.ShapeDtypeStruct((B, S, D), F32),
        scratch_shapes=[pltpu.VMEM((tm + 2 * HALO_ROWS, D), BF16), pltpu.VMEM((tm, D_FF), BF16)],
        compiler_params=_params("parallel", "parallel"),
        name="ffn",
    )(x, x, x, g, w_up, conv_w, conv_b, w_down)


def _rope_tables(S):
    inv = ROPE_THETA ** (-(jnp.arange(ROT_HALF, dtype=F32) * 2.0 / (2 * ROT_HALF)))
    ang = jnp.arange(S, dtype=F32)[:, None] * inv[None, :]
    cos, sin = jnp.cos(ang), jnp.sin(ang)
    rest = HEAD_DIM - 2 * ROT_HALF
    one, zero = jnp.ones((S, rest), F32), jnp.zeros((S, rest), F32)
    zhalf = jnp.zeros((S, ROT_HALF), F32)
    per_head = (jnp.concatenate([cos, cos, one], axis=1),
                jnp.concatenate([zhalf, sin, zero], axis=1),
                jnp.concatenate([-sin, zhalf, zero], axis=1))
    return tuple(jnp.tile(t, (1, LANES // HEAD_DIM)) for t in per_head)


def _pair_gain(g, scale=1.0):
    g = jnp.atleast_2d(g.astype(F32) * scale)
    return jnp.tile(g, (1, LANES // HEAD_DIM))


def _group_ones(width):
    head = jnp.arange(width) // HEAD_DIM
    return (head[:, None] == head[None, :]).astype(BF16)


def _score_bound(q_gain, k_gain):
    return (NORM_SLACK * HEAD_DIM * Q_SCALE * jnp.max(jnp.abs(q_gain.astype(F32)))
            * jnp.max(jnp.abs(k_gain.astype(F32))))


def _a_weight(w):
    d = w.shape[0]
    n = A_HEADS * HEAD_DIM
    pair = lambda t: t.reshape(d, 2, A_HEADS, HEAD_DIM).transpose(0, 2, 1, 3).reshape(d, 2 * n)
    return jnp.concatenate([pair(w[:, :2 * n]), pair(w[:, 2 * n:4 * n]), w[:, 4 * n:]], axis=1).astype(BF16)


def _trunk(x, mem, P):
    S = x.shape[1]
    rope = _rope_tables(S)
    ones = _group_ones(MXU_COLS)
    n_layers = P["norm_mix"].shape[0]
    for i in range(n_layers):
        j = i // 2
        row = lambda name: P[name][i][None, :].astype(F32)
        mk, mv = _mem_kv(mem, row("norm_mem"), P["w_mem_kv"][i].astype(BF16), _group_ones(LANES),
                         _pair_gain(P["xk_norm"][i]))
        xg = _pair_gain(P["xq_norm"][i], Q_SCALE)
        if i % 2 == 0:
            qg = jnp.tile(_pair_gain(P["a_q_norm"][j], Q_SCALE), (A_HEADS, 1))
            kg = jnp.tile(_pair_gain(P["a_k_norm"][j]), (A_HEADS, 1))
            q, k, v, xq = _inproj(x, row("norm_mix"), _a_weight(P["a_w_in"][j]), ones, rope,
                                  qg, kg, xg, A_HEADS)
            lam_init = 0.8 - 0.6 * math.exp(-0.3 * i)
            subln = P["a_subln"][j][None, :].astype(F32) * (1.0 - lam_init)
            mixed = _diff_attn(q[:, 0], k[:, 0], v[:, 0], P["a_lambda"][j].astype(F32), subln, lam_init,
                               _score_bound(P["a_q_norm"][j], P["a_k_norm"][j]))
            w_out = P["a_w_out"][j]
        else:
            pairs = B_HEADS // 2
            width = B_HEADS * HEAD_DIM
            w_in = P["b_w_in"][j]
            outs, lses = [], []
            for gi, (window, dil) in enumerate(B_GROUPS):
                cols = [w_in[:, s * len(B_GROUPS) * width + gi * width:][:, :width] for s in range(3)]
                if gi == 0:
                    cols.append(w_in[:, 3 * len(B_GROUPS) * width:])
                qg = jnp.tile(_pair_gain(P["b_q_norm"][j][gi], Q_SCALE), (pairs, 1))
                kg = jnp.tile(_pair_gain(P["b_k_norm"][j][gi]), (pairs, 1))
                q, k, v, *rest = _inproj(x, row("norm_mix"), jnp.concatenate(cols, axis=1).astype(BF16),
                                         ones, rope, qg, kg, xg, pairs, dil=dil, with_xq=(gi == 0))
                if gi == 0:
                    xq = rest[0]
                o, l = _band_attn(q, k, v, (window // 2) // dil,
                                  _score_bound(P["b_q_norm"][j][gi], P["b_k_norm"][j][gi]))
                outs.append(o)
                lses.append(l)
            mixed = (outs, lses)
            w_out = P["b_w_out"][j]
        x = _outproj(mixed, xq, mk, mv, x, w_out.astype(BF16))
        x = _ffn(x, row("norm_ffn"), P["w_up"][i].astype(BF16), P["conv_w"][i].astype(F32),
                 P["conv_b"][i][None, :].astype(F32), P["w_down"][i].astype(BF16))
    return x


def kernel(x_prompt, x_sample, mem_prompt, mem_sample, norm_mix, norm_mem, w_mem_kv, xq_norm, xk_norm, a_w_in, a_w_out, a_q_norm, a_k_norm, a_lambda, a_subln, b_w_in, b_w_out, b_q_norm, b_k_norm, norm_ffn, w_up, conv_w, conv_b, w_down):
    P = dict(norm_mix=norm_mix, norm_mem=norm_mem, w_mem_kv=w_mem_kv, xq_norm=xq_norm, xk_norm=xk_norm,
             a_w_in=a_w_in, a_w_out=a_w_out, a_q_norm=a_q_norm, a_k_norm=a_k_norm, a_lambda=a_lambda,
             a_subln=a_subln, b_w_in=b_w_in, b_w_out=b_w_out, b_q_norm=b_q_norm, b_k_norm=b_k_norm,
             norm_ffn=norm_ffn, w_up=w_up, conv_w=conv_w, conv_b=conv_b, w_down=w_down)
    return _trunk(x_prompt, mem_prompt, P), _trunk(x_sample, mem_sample, P)
```

```python
import functools
import math

import jax
import jax.numpy as jnp
from jax import lax
from jax.experimental import pallas as pl
from jax.experimental.pallas import tpu as pltpu

F32 = jnp.float32
BF16 = jnp.bfloat16

LANES = 128
HEAD_DIM = 64
ROT_HALF = 8
A_HEADS = 8
B_HEADS = 8
B_GROUPS = ((128, 1), (512, 4), (2048, 16))
X_HEADS = 4
D_FF = 2816
ROPE_THETA = 500000.0
EPS = 1e-6
NEG_INF = -1e30
LOG2E = math.log2(math.e)
Q_SCALE = LOG2E / math.sqrt(HEAD_DIM)

V7X_VMEM_BYTES = 64 * 1024 * 1024
VMEM_LIMIT = V7X_VMEM_BYTES * 7 // 8

ROW_TILE = 1024
INPROJ_ROW_TILE = 1024
OUTPROJ_ROW_TILE = 1024
FFN_ROW_TILE = 1024
Q_TILE_A = 2048
Q_SUB_A = 256
KEY_CHUNK = 512
SAFE_EXP2_RANGE = 50.0
NORM_SLACK = 1.02
Q_TILE_B = 128
MXU_COLS = 256


def _params(*sem):
    return pltpu.CompilerParams(dimension_semantics=sem, vmem_limit_bytes=VMEM_LIMIT)


def _full(shape):
    return pl.BlockSpec(shape, lambda *_: (0,) * len(shape))


def _resident(shape):
    return pl.BlockSpec(shape, lambda *_: (0,) * len(shape), pipeline_mode=pl.Buffered(1))


def _rms_rows(x, g):
    ms = jnp.mean(x * x, axis=-1, keepdims=True)
    return x * lax.rsqrt(ms + EPS) * g


def _lane_tiles(x):
    return [x[:, t * LANES:(t + 1) * LANES] for t in range(x.shape[1] // LANES)]


def _scale_rows(y, ssq, gain):
    return y * lax.rsqrt(ssq * (1.0 / HEAD_DIM) + EPS) * gain


def _pair_norm(y, group_ones, gain):
    ssq = jnp.dot((y * y).astype(BF16), group_ones, preferred_element_type=F32)
    return _scale_rows(y, ssq, gain)


def _rope(y, cos, sin_lo, sin_hi):
    return (y * cos + pltpu.roll(y, ROT_HALF, axis=1) * sin_lo
            + pltpu.roll(y, LANES - ROT_HALF, axis=1) * sin_hi)


def _split_heads(q):
    lane = lax.broadcasted_iota(jnp.int32, q.shape, 1)
    zero = jnp.zeros_like(q)
    return jnp.where(lane < HEAD_DIM, q, zero), jnp.where(lane >= HEAD_DIM, q, zero)


def _merge_heads(o_lo, o_hi):
    lane = lax.broadcasted_iota(jnp.int32, o_lo.shape, 1)
    return jnp.where(lane < HEAD_DIM, o_lo, o_hi)


def _scores(q, k):
    return lax.dot_general(q, k, (((1,), (1,)), ((), ())), preferred_element_type=F32)


def _inproj_kernel(nq, nk, nv, nx, x_ref, g_ref, w_ref, ones_ref, cos_ref, slo_ref, shi_ref,
                   qg_ref, kg_ref, xg_ref, *out_and_scratch):
    q_ref, k_ref, v_ref = out_and_scratch[:3]
    dil, tl = q_ref.shape[1], q_ref.shape[3]
    if dil == 1:
        h_ref = out_and_scratch[-1]
        h_ref[...] = _rms_rows(x_ref[0], g_ref[...]).astype(BF16)
    else:
        stage_ref, h_ref = out_and_scratch[-2:]
        h = _rms_rows(x_ref[0], g_ref[...])
        for c in range(stage_ref.shape[0]):
            stage_ref[c] = h[:, c * LANES:(c + 1) * LANES]
        for r in range(dil):
            for c in range(stage_ref.shape[0]):
                h_ref[r * tl:(r + 1) * tl, c * LANES:(c + 1) * LANES] = (
                    stage_ref[c, pl.ds(r, tl, stride=dil), :].astype(BF16))

    def put(ref, p, y):
        for r in range(dil):
            ref[0, r, p] = y[r * tl:(r + 1) * tl].astype(BF16)

    ones = ones_ref[...]
    cos, slo, shi = cos_ref[...], slo_ref[...], shi_ref[...]
    pairs_per_chunk = MXU_COLS // LANES
    n_chunks = (nq + nk + nv + nx) // pairs_per_chunk

    def project(ch):
        return jnp.dot(h_ref[...], w_ref[:, ch * MXU_COLS:(ch + 1) * MXU_COLS],
                       preferred_element_type=F32)

    acc_next = project(0)
    for ch in range(n_chunks):
        acc = acc_next
        if ch + 1 < n_chunks:
            acc_next = project(ch + 1)
        ys = _lane_tiles(acc)
        first = ch * pairs_per_chunk
        if first < nq + nk or first >= nq + nk + nv:
            ssq = _lane_tiles(jnp.dot((acc * acc).astype(BF16), ones, preferred_element_type=F32))
        for half, y in enumerate(ys):
            p = first + half
            if p < nq + nk:
                ref, gain = (q_ref, qg_ref) if p < nq else (k_ref, kg_ref)
                p = p if p < nq else p - nq
                put(ref, p, _rope(_scale_rows(y, ssq[half], gain[p:p + 1, :]), cos, slo, shi))
            elif p < nq + nk + nv:
                put(v_ref, p - nq - nk, y)
            else:
                put(out_and_scratch[3], p - nq - nk - nv, _scale_rows(y, ssq[half], xg_ref[...]))


def _inproj(x, g, w, ones, rope, qg, kg, xg, nv, dil=1, with_xq=True):
    B, S, D = x.shape
    nq, nk, nx = qg.shape[0], kg.shape[0], (X_HEADS // 2 if with_xq else 0)
    tm = INPROJ_ROW_TILE
    tl = tm // dil
    rope = [t.reshape(S // tm, tl, dil, LANES).transpose(0, 2, 1, 3).reshape(S, LANES) for t in rope]
    tab = pl.BlockSpec((tm, LANES), lambda b, i: (i, 0))
    sizes = [nq, nk, nv] + ([nx] if with_xq else [])
    scratch = [pltpu.VMEM((tm, D), BF16)]
    if dil > 1:
        scratch.insert(0, pltpu.VMEM((D // LANES, tm, LANES), F32))
    return pl.pallas_call(
        functools.partial(_inproj_kernel, nq, nk, nv, nx),
        grid=(B, S // tm),
        in_specs=[pl.BlockSpec((1, tm, D), lambda b, i: (b, i, 0)), _full(g.shape), _full(w.shape),
                  _full(ones.shape), tab, tab, tab, _full(qg.shape), _full(kg.shape), _full(xg.shape)],
        out_specs=[pl.BlockSpec((1, dil, n, tl, LANES), lambda b, i: (b, 0, 0, i, 0)) for n in sizes],
        out_shape=[jax.ShapeDtypeStruct((B, dil, n, S // dil, LANES), BF16) for n in sizes],
        scratch_shapes=scratch,
        compiler_params=_params("parallel", "parallel"),
        name="inproj",
    )(x, g, w, ones, *rope, qg, kg, xg)


def _exp2_rows(s):
    e = jnp.exp2(s - jnp.max(s, axis=-1, keepdims=True))
    return e.astype(BF16), jnp.sum(e, axis=-1, keepdims=True)


def _diff_attn_kernel(lam_init, shifted, q_ref, k_ref, v_ref, lam_ref, sg_ref, o_ref, s_ref):
    seq = k_ref.shape[2]
    ring, _, rows, _ = s_ref.shape
    kc = KEY_CHUNK
    n_sub = q_ref.shape[2] // rows
    lp = lam_ref[...]
    lam = (jnp.exp(jnp.sum(lp[0:1] * lp[1:2], axis=-1, keepdims=True))
           - jnp.exp(jnp.sum(lp[2:3] * lp[3:4], axis=-1, keepdims=True)) + lam_init)
    wide = lambda col: jnp.broadcast_to(col, (rows, LANES))
    lag = 1 if shifted else 0
    row_max, coef = {}, {}
    for it in range(n_sub + lag + 1):
        sb_qk, sb_exp, sb_pv = it, it - lag, it - lag - 1
        do_qk, do_exp, do_pv = sb_qk < n_sub, 0 <= sb_exp < n_sub, 0 <= sb_pv
        if do_qk:
            qs = _split_heads(q_ref[0, 0, sb_qk * rows:(sb_qk + 1) * rows, :])
            m_wide = [None, None]
        if do_exp:
            l_wide = [None, None]
        acc = None
        for c0 in range(0, seq, kc):
            cols = slice(c0, c0 + kc)
            if do_qk:
                for h in range(2):
                    s = _scores(qs[h], k_ref[0, 0, cols, :])
                    if shifted:
                        s_ref[sb_qk % ring, h, :, cols] = s
                        m_wide[h] = functools.reduce(
                            jnp.maximum, _lane_tiles(s) + ([] if m_wide[h] is None else [m_wide[h]]))
                    else:
                        e = [jnp.exp2(t) for t in _lane_tiles(s)]
                        s_ref[sb_qk % ring, h, :, cols] = jnp.concatenate(e, axis=1)
                        l_wide[h] = functools.reduce(jnp.add, e + ([] if l_wide[h] is None else [l_wide[h]]))
            if shifted and do_exp:
                for h in range(2):
                    e = [jnp.exp2(t - row_max[sb_exp][h]) for t in _lane_tiles(s_ref[sb_exp % ring, h, :, cols])]
                    s_ref[sb_exp % ring, h, :, cols] = jnp.concatenate(e, axis=1)
                    l_wide[h] = functools.reduce(jnp.add, e + ([] if l_wide[h] is None else [l_wide[h]]))
            if do_pv:
                c1, c2 = coef[sb_pv]
                diff = [t1 * c1 - t2 * c2 for t1, t2 in zip(_lane_tiles(s_ref[sb_pv % ring, 0, :, cols]),
                                                           _lane_tiles(s_ref[sb_pv % ring, 1, :, cols]))]
                pv = jnp.dot(jnp.concatenate(diff, axis=1).astype(BF16), v_ref[0, 0, cols, :],
                             preferred_element_type=F32)
                acc = pv if acc is None else acc + pv
        if shifted and do_qk:
            row_max[sb_qk] = [wide(jnp.max(m, axis=-1, keepdims=True)) for m in m_wide]
        if do_exp:
            l1, l2 = [jnp.sum(l, axis=-1, keepdims=True) for l in l_wide]
            coef[sb_exp] = (wide(1.0 / l1), wide(lam / l2))
        if do_pv:
            o_ref[0, sb_pv * rows:(sb_pv + 1) * rows, :] = _rms_rows(acc, sg_ref[...]).astype(BF16)


def _diff_attn_call(shifted, lam_init, q, k, v, lam_p, subln):
    B, H, S, _ = q.shape
    tq = Q_TILE_A
    kv = pl.BlockSpec((1, 1, S, LANES), lambda b, h, i: (b, h, 0, 0))
    stages_in_flight = 3 if shifted else 2
    return pl.pallas_call(
        functools.partial(_diff_attn_kernel, lam_init, shifted),
        grid=(B, H, S // tq),
        in_specs=[pl.BlockSpec((1, 1, tq, LANES), lambda b, h, i: (b, h, i, 0)), kv, kv,
                  _full(lam_p.shape), _full(subln.shape)],
        out_specs=pl.BlockSpec((1, tq, LANES), lambda b, h, i: (b, i, h)),
        out_shape=jax.ShapeDtypeStruct((B, S, H * LANES), BF16),
        scratch_shapes=[pltpu.VMEM((stages_in_flight, 2, Q_SUB_A, S), F32)],
        compiler_params=_params("parallel", "parallel", "parallel"),
        name="diff_attn_shifted" if shifted else "diff_attn",
    )(q, k, v, lam_p, subln)


def _diff_attn(q, k, v, lam_p, subln, lam_init, score_bound):
    return lax.cond(score_bound <= SAFE_EXP2_RANGE,
                    functools.partial(_diff_attn_call, False, lam_init),
                    functools.partial(_diff_attn_call, True, lam_init),
                    q, k, v, lam_p, subln)


def _band_attn_kernel(radius, shifted, q_ref, k_ref, v_ref, o_ref, l_ref):
    _, res, pairs, rows, _ = q_ref.shape
    L = k_ref.shape[3]
    tq = min(Q_TILE_B, L)
    span = min(L, tq + 2 * radius)
    row0 = pl.program_id(2) * rows
    col = lax.broadcasted_iota(jnp.int32, (tq, span), 1)
    qrow = lax.broadcasted_iota(jnp.int32, (tq, span), 0)
    for r in range(res):
        for blk in range(rows // tq):
            q0 = row0 + blk * tq
            start = pl.multiple_of(jnp.clip(q0 - radius, 0, L - span), radius)
            valid = jnp.abs(col - qrow + (start - q0)) <= radius
            rs = slice(blk * tq, (blk + 1) * tq)
            scores = [[_scores(qh, k_ref[0, r, p, pl.ds(start, span), :])
                       for qh in _split_heads(q_ref[0, r, p, rs, :])] for p in range(pairs)]
            probs = []
            for p in range(pairs):
                per_head = []
                for s in scores[p]:
                    s = jnp.where(valid, s, NEG_INF)
                    m = jnp.max(s, axis=-1, keepdims=True) if shifted else 0.0
                    e = jnp.exp2(s - m) if shifted else jnp.exp2(s)
                    z = jnp.sum(e, axis=-1, keepdims=True)
                    per_head.append((e.astype(BF16), z, m + jnp.log2(z)))
                probs.append(per_head)
            for p in range(pairs):
                vw = v_ref[0, r, p, pl.ds(start, span), :]
                outs = [jnp.dot(e, vw, preferred_element_type=F32) / z for e, z, _ in probs[p]]
                lses = [jnp.broadcast_to(lse, (tq, LANES)) for _, _, lse in probs[p]]
                o_ref[0, r, p, rs, :] = _merge_heads(*outs).astype(BF16)
                l_ref[0, r, p, rs, :] = _merge_heads(*lses)


def _band_attn_call(radius, shifted, q, k, v):
    B, dil, pairs, L, _ = q.shape
    rows = min(L, ROW_TILE)
    res = ROW_TILE // rows
    kv = pl.BlockSpec((1, res, pairs, L, LANES), lambda b, r, i: (b, r, 0, 0, 0))
    tile = pl.BlockSpec((1, res, pairs, rows, LANES), lambda b, r, i: (b, r, 0, i, 0))
    return pl.pallas_call(
        functools.partial(_band_attn_kernel, radius, shifted),
        grid=(B, dil // res, L // rows),
        in_specs=[tile, kv, kv],
        out_specs=[tile, tile],
        out_shape=[jax.ShapeDtypeStruct(q.shape, BF16), jax.ShapeDtypeStruct(q.shape, F32)],
        compiler_params=_params("parallel", "parallel", "parallel"),
        name="band_attn_shifted" if shifted else "band_attn",
    )(q, k, v)


def _band_attn(q, k, v, radius, score_bound):
    return lax.cond(score_bound <= SAFE_EXP2_RANGE,
                    functools.partial(_band_attn_call, radius, False),
                    functools.partial(_band_attn_call, radius, True), q, k, v)


def _group_mix_into(o_refs, l_refs, stage_ref, dst_ref):
    def token_order(ref, p, slot):
        dil, tl = ref.shape[1], ref.shape[3]
        if dil == 1:
            return ref[0, 0, p].astype(F32)
        for r in range(dil):
            stage_ref[slot, pl.ds(r, tl, stride=dil), :] = ref[0, r, p].astype(F32)
        return stage_ref[slot]

    for p in range(o_refs[0].shape[2]):
        ls = [token_order(l, p, 2 * g) for g, l in enumerate(l_refs)]
        os_ = [token_order(o, p, 2 * g + 1) for g, o in enumerate(o_refs)]
        m = functools.reduce(jnp.maximum, ls)
        ws = [jnp.exp2(l - m) for l in ls]
        num = functools.reduce(jnp.add, [w * o for w, o in zip(ws, os_)])
        dst_ref[:, p * LANES:(p + 1) * LANES] = (num / functools.reduce(jnp.add, ws)).astype(BF16)


def _mem_kv_kernel(m_ref, g_ref, w_ref, ones_ref, kg_ref, k_ref, v_ref):
    h = _rms_rows(m_ref[0], g_ref[...]).astype(BF16)
    kv = jnp.dot(h, w_ref[...], preferred_element_type=F32)
    half = kv.shape[1] // 2
    for p in range(half // LANES):
        y = _pair_norm(kv[:, p * LANES:(p + 1) * LANES], ones_ref[...], kg_ref[...])
        k_ref[0, :, p * LANES:(p + 1) * LANES] = y.astype(BF16)
    v_ref[0] = kv[:, half:].astype(BF16)


def _mem_kv(mem, g, w, ones, kg):
    B, M, D = mem.shape
    width = w.shape[1] // 2
    blk = pl.BlockSpec((1, M, width), lambda b: (b, 0, 0))
    return pl.pallas_call(
        _mem_kv_kernel,
        grid=(B,),
        in_specs=[pl.BlockSpec((1, M, D), lambda b: (b, 0, 0)), _full(g.shape), _full(w.shape),
                  _full(ones.shape), _full(kg.shape)],
        out_specs=[blk, blk],
        out_shape=[jax.ShapeDtypeStruct((B, M, width), BF16)] * 2,
        compiler_params=_params("parallel"),
        name="mem_kv",
    )(mem, g, w, ones, kg)


def _cross_attn_into(q_ref, k_ref, v_ref, dst_ref, col0):
    pairs = range(q_ref.shape[2])
    cols = [slice(p * LANES, (p + 1) * LANES) for p in pairs]
    scores = [[_scores(qh, k_ref[0, :, cols[p]]) for qh in _split_heads(q_ref[0, 0, p])] for p in pairs]
    probs = [[_exp2_rows(s) for s in scores[p]] for p in pairs]
    for p in pairs:
        v = v_ref[0, :, cols[p]]
        outs = [jnp.dot(e, v, preferred_element_type=F32) / l for e, l in probs[p]]
        dst_ref[:, col0 + p * LANES:col0 + (p + 1) * LANES] = _merge_heads(*outs).astype(BF16)


def _outproj_kernel(n_groups, *refs):
    if n_groups:
        o_refs, l_refs = refs[:n_groups], refs[n_groups:2 * n_groups]
        xq_ref, mk_ref, mv_ref, x_ref, w_ref, out_ref, lhs_ref, stage_ref = refs[2 * n_groups:]
        mix_width = o_refs[0].shape[2] * LANES
        _group_mix_into(o_refs, l_refs, stage_ref, lhs_ref)
        _cross_attn_into(xq_ref, mk_ref, mv_ref, lhs_ref, mix_width)
        out_ref[0] = x_ref[0] + jnp.dot(lhs_ref[...], w_ref[...], preferred_element_type=F32)
    else:
        mix_ref, xq_ref, mk_ref, mv_ref, x_ref, w_ref, out_ref, lhs_ref = refs
        mix_width = mix_ref.shape[2]
        _cross_attn_into(xq_ref, mk_ref, mv_ref, lhs_ref, 0)
        out_ref[0] = (x_ref[0]
                      + jnp.dot(mix_ref[0], w_ref[:mix_width, :], preferred_element_type=F32)
                      + jnp.dot(lhs_ref[...], w_ref[mix_width:, :], preferred_element_type=F32))


def _outproj(mixed, xq, mk, mv, x, w):
    B, S, D = x.shape
    M, cross_width = mk.shape[1:]
    tm = OUTPROJ_ROW_TILE
    row = lambda c: pl.BlockSpec((1, tm, c), lambda b, i: (b, i, 0))
    kv = pl.BlockSpec((1, M, cross_width), lambda b, i: (b, 0, 0))
    common = [pl.BlockSpec((1, 1, xq.shape[2], tm, LANES), lambda b, i: (b, 0, 0, i, 0)), kv, kv,
              row(D), _full(w.shape)]
    if isinstance(mixed, tuple):
        groups = [*mixed[0], *mixed[1]]
        pairs = groups[0].shape[2]
        group_spec = lambda t: pl.BlockSpec((1, t.shape[1], pairs, tm // t.shape[1], LANES),
                                            lambda b, i: (b, 0, 0, i, 0))
        in_specs = [group_spec(t) for t in groups] + common
        args = groups
        scratch = [pltpu.VMEM((tm, pairs * LANES + cross_width), BF16),
                   pltpu.VMEM((len(groups), tm, LANES), F32)]
        n_groups = len(mixed[0])
    else:
        in_specs = [row(mixed.shape[2])] + common
        args = [mixed]
        scratch = [pltpu.VMEM((tm, cross_width), BF16)]
        n_groups = 0
    return pl.pallas_call(
        functools.partial(_outproj_kernel, n_groups),
        grid=(B, S // tm),
        in_specs=in_specs,
        out_specs=row(D),
        out_shape=jax.ShapeDtypeStruct((B, S, D), F32),
        scratch_shapes=scratch,
        compiler_params=_params("parallel", "parallel"),
        name="outproj",
    )(*args, xq, mk, mv, x, w)


FF_CHUNK = 256


HALO_ROWS = 16


def _ffn_kernel(x_ref, prev_ref, next_ref, g_ref, wu_ref, cw_ref, cb_ref, wd_ref, o_ref,
                h_ref, act_ref):
    tm = x_ref.shape[1]
    i, n = pl.program_id(1), pl.num_programs(1)
    g = g_ref[...]
    h_ref[0:HALO_ROWS] = (_rms_rows(prev_ref[0], g) * (i > 0).astype(F32)).astype(BF16)
    h_ref[HALO_ROWS:HALO_ROWS + tm] = _rms_rows(x_ref[0], g).astype(BF16)
    h_ref[HALO_ROWS + tm:] = (_rms_rows(next_ref[0], g) * (i < n - 1).astype(F32)).astype(BF16)
    main = slice(HALO_ROWS, HALO_ROWS + tm)

    def project(c0):
        return [jnp.dot(h_ref[...], wu_ref[:, col:col + FF_CHUNK], preferred_element_type=F32)
                for col in (c0, D_FF + c0)]

    def conv(u, col):
        cols = slice(col, col + FF_CHUNK)
        before = pltpu.roll(u, 1, axis=0)[main]
        after = pltpu.roll(u, u.shape[0] - 1, axis=0)[main]
        return (before * cw_ref[0:1, cols] + u[main] * cw_ref[1:2, cols] + after * cw_ref[2:3, cols]
                + cb_ref[:, cols])

    starts = list(range(0, D_FF, FF_CHUNK))
    u_next = project(starts[0])
    for idx, c0 in enumerate(starts):
        u_a, u_b = u_next
        if idx + 1 < len(starts):
            u_next = project(starts[idx + 1])
        a, b = conv(u_a, c0), conv(u_b, D_FF + c0)
        act_ref[:, c0:c0 + FF_CHUNK] = (a / (1.0 + jnp.exp(-a)) * b).astype(BF16)
    o_ref[0] = x_ref[0] + jnp.dot(act_ref[...], wd_ref[...], preferred_element_type=F32)


def _ffn(x, g, w_up, conv_w, conv_b, w_down):
    B, S, D = x.shape
    tm = FFN_ROW_TILE
    halo_per_tile = tm // HALO_ROWS
    last_halo = S // HALO_ROWS - 1
    tile = pl.BlockSpec((1, tm, D), lambda b, i: (b, i, 0))
    return pl.pallas_call(
        _ffn_kernel,
        grid=(B, S // tm),
        in_specs=[tile,
                  pl.BlockSpec((1, HALO_ROWS, D), lambda b, i: (b, jnp.maximum(i * halo_per_tile - 1, 0), 0)),
                  pl.BlockSpec((1, HALO_ROWS, D),
                               lambda b, i: (b, jnp.minimum((i + 1) * halo_per_tile, last_halo), 0)),
                  _full(g.shape), _resident(w_up.shape), _full(conv_w.shape), _full(conv_b.shape),
                  _resident(w_down.shape)],
        out_specs=tile,
        out_shape=jax.ShapeDtypeStruct((B, S, D), F32),
        scratch_shapes=[pltpu.VMEM((tm + 2 * HALO_ROWS, D), BF16), pltpu.VMEM((tm, D_FF), BF16)],
        compiler_params=_params("parallel", "parallel"),
        name="ffn",
    )(x, x, x, g, w_up, conv_w, conv_b, w_down)


def _rope_tables(S):
    inv = ROPE_THETA ** (-(jnp.arange(ROT_HALF, dtype=F32) * 2.0 / (2 * ROT_HALF)))
    ang = jnp.arange(S, dtype=F32)[:, None] * inv[None, :]
    cos, sin = jnp.cos(ang), jnp.sin(ang)
    rest = HEAD_DIM - 2 * ROT_HALF
    one, zero = jnp.ones((S, rest), F32), jnp.zeros((S, rest), F32)
    zhalf = jnp.zeros((S, ROT_HALF), F32)
    per_head = (jnp.concatenate([cos, cos, one], axis=1),
                jnp.concatenate([zhalf, sin, zero], axis=1),
                jnp.concatenate([-sin, zhalf, zero], axis=1))
    return tuple(jnp.tile(t, (1, LANES // HEAD_DIM)) for t in per_head)


def _pair_gain(g, scale=1.0):
    g = jnp.atleast_2d(g.astype(F32) * scale)
    return jnp.tile(g, (1, LANES // HEAD_DIM))


def _group_ones(width):
    head = jnp.arange(width) // HEAD_DIM
    return (head[:, None] == head[None, :]).astype(BF16)


def _score_bound(q_gain, k_gain):
    return (NORM_SLACK * HEAD_DIM * Q_SCALE * jnp.max(jnp.abs(q_gain.astype(F32)))
            * jnp.max(jnp.abs(k_gain.astype(F32))))


def _a_weight(w):
    d = w.shape[0]
    n = A_HEADS * HEAD_DIM
    pair = lambda t: t.reshape(d, 2, A_HEADS, HEAD_DIM).transpose(0, 2, 1, 3).reshape(d, 2 * n)
    return jnp.concatenate([pair(w[:, :2 * n]), pair(w[:, 2 * n:4 * n]), w[:, 4 * n:]], axis=1).astype(BF16)


def _trunk(x, mem, P):
    S = x.shape[1]
    rope = _rope_tables(S)
    ones = _group_ones(MXU_COLS)
    n_layers = P["norm_mix"].shape[0]
    for i in range(n_layers):
        j = i // 2
        row = lambda name: P[name][i][None, :].astype(F32)
        mk, mv = _mem_kv(mem, row("norm_mem"), P["w_mem_kv"][i].astype(BF16), _group_ones(LANES),
                         _pair_gain(P["xk_norm"][i]))
        xg = _pair_gain(P["xq_norm"][i], Q_SCALE)
        if i % 2 == 0:
            qg = jnp.tile(_pair_gain(P["a_q_norm"][j], Q_SCALE), (A_HEADS, 1))
            kg = jnp.tile(_pair_gain(P["a_k_norm"][j]), (A_HEADS, 1))
            q, k, v, xq = _inproj(x, row("norm_mix"), _a_weight(P["a_w_in"][j]), ones, rope,
                                  qg, kg, xg, A_HEADS)
            lam_init = 0.8 - 0.6 * math.exp(-0.3 * i)
            subln = P["a_subln"][j][None, :].astype(F32) * (1.0 - lam_init)
            mixed = _diff_attn(q[:, 0], k[:, 0], v[:, 0], P["a_lambda"][j].astype(F32), subln, lam_init,
                               _score_bound(P["a_q_norm"][j], P["a_k_norm"][j]))
            w_out = P["a_w_out"][j]
        else:
            pairs = B_HEADS // 2
            width = B_HEADS * HEAD_DIM
            w_in = P["b_w_in"][j]
            outs, lses = [], []
            for gi, (window, dil) in enumerate(B_GROUPS):
                cols = [w_in[:, s * len(B_GROUPS) * width + gi * width:][:, :width] for s in range(3)]
                if gi == 0:
                    cols.append(w_in[:, 3 * len(B_GROUPS) * width:])
                qg = jnp.tile(_pair_gain(P["b_q_norm"][j][gi], Q_SCALE), (pairs, 1))
                kg = jnp.tile(_pair_gain(P["b_k_norm"][j][gi]), (pairs, 1))
                q, k, v, *rest = _inproj(x, row("norm_mix"), jnp.concatenate(cols, axis=1).astype(BF16),
                                         ones, rope, qg, kg, xg, pairs, dil=dil, with_xq=(gi == 0))
                if gi == 0:
                    xq = rest[0]
                o, l = _band_attn(q, k, v, (window // 2) // dil,
                                  _score_bound(P["b_q_norm"][j][gi], P["b_k_norm"][j][gi]))
                outs.append(o)
                lses.append(l)
            mixed = (outs, lses)
            w_out = P["b_w_out"][j]
        x = _outproj(mixed, xq, mk, mv, x, w_out.astype(BF16))
        x = _ffn(x, row("norm_ffn"), P["w_up"][i].astype(BF16), P["conv_w"][i].astype(F32),
                 P["conv_b"][i][None, :].astype(F32), P["w_down"][i].astype(BF16))
    return x


def kernel(x_prompt, x_sample, mem_prompt, mem_sample, norm_mix, norm_mem, w_mem_kv, xq_norm, xk_norm, a_w_in, a_w_out, a_q_norm, a_k_norm, a_lambda, a_subln, b_w_in, b_w_out, b_q_norm, b_k_norm, norm_ffn, w_up, conv_w, conv_b, w_down):
    P = dict(norm_mix=norm_mix, norm_mem=norm_mem, w_mem_kv=w_mem_kv, xq_norm=xq_norm, xk_norm=xk_norm,
             a_w_in=a_w_in, a_w_out=a_w_out, a_q_norm=a_q_norm, a_k_norm=a_k_norm, a_lambda=a_lambda,
             a_subln=a_subln, b_w_in=b_w_in, b_w_out=b_w_out, b_q_norm=b_q_norm, b_k_norm=b_k_norm,
             norm_ffn=norm_ffn, w_up=w_up, conv_w=conv_w, conv_b=conv_b, w_down=w_down)
    return _trunk(x_prompt, mem_prompt, P), _trunk(x_sample, mem_sample, P)
```

```python
import functools
import math

import jax
import jax.numpy as jnp
from jax import lax
from jax.experimental import pallas as pl
from jax.experimental.pallas import tpu as pltpu

F32 = jnp.float32
BF16 = jnp.bfloat16

LANES = 128
HEAD_DIM = 64
ROT_HALF = 8
A_HEADS = 8
B_HEADS = 8
B_GROUPS = ((128, 1), (512, 4), (2048, 16))
X_HEADS = 4
D_FF = 2816
ROPE_THETA = 500000.0
EPS = 1e-6
NEG_INF = -1e30
LOG2E = math.log2(math.e)
Q_SCALE = LOG2E / math.sqrt(HEAD_DIM)

V7X_VMEM_BYTES = 64 * 1024 * 1024
VMEM_LIMIT = V7X_VMEM_BYTES * 7 // 8

ROW_TILE = 2048
INPROJ_ROW_TILE = 1024
OUTPROJ_ROW_TILE = 1024
FFN_ROW_TILE = 1024
Q_TILE_A = 2048
Q_SUB_A = 256
KEY_CHUNK = 512
SAFE_EXP2_RANGE = 50.0
NORM_SLACK = 1.02
Q_TILE_B = 128
MXU_COLS = 256


def _params(*sem):
    return pltpu.CompilerParams(dimension_semantics=sem, vmem_limit_bytes=VMEM_LIMIT)


def _full(shape):
    return pl.BlockSpec(shape, lambda *_: (0,) * len(shape))


def _resident(shape):
    return pl.BlockSpec(shape, lambda *_: (0,) * len(shape), pipeline_mode=pl.Buffered(1))


def _rms_rows(x, g):
    ms = jnp.mean(x * x, axis=-1, keepdims=True)
    return x * lax.rsqrt(ms + EPS) * g


def _lane_tiles(x):
    return [x[:, t * LANES:(t + 1) * LANES] for t in range(x.shape[1] // LANES)]


def _scale_rows(y, ssq, gain):
    return y * lax.rsqrt(ssq * (1.0 / HEAD_DIM) + EPS) * gain


def _pair_norm(y, group_ones, gain):
    ssq = jnp.dot((y * y).astype(BF16), group_ones, preferred_element_type=F32)
    return _scale_rows(y, ssq, gain)


def _rope(y, cos, sin_lo, sin_hi):
    return (y * cos + pltpu.roll(y, ROT_HALF, axis=1) * sin_lo
            + pltpu.roll(y, LANES - ROT_HALF, axis=1) * sin_hi)


def _split_heads(q):
    lane = lax.broadcasted_iota(jnp.int32, q.shape, 1)
    zero = jnp.zeros_like(q)
    return jnp.where(lane < HEAD_DIM, q, zero), jnp.where(lane >= HEAD_DIM, q, zero)


def _merge_heads(o_lo, o_hi):
    lane = lax.broadcasted_iota(jnp.int32, o_lo.shape, 1)
    return jnp.where(lane < HEAD_DIM, o_lo, o_hi)


def _scores(q, k):
    return lax.dot_general(q, k, (((1,), (1,)), ((), ())), preferred_element_type=F32)


def _inproj_kernel(nq, nk, nv, nx, x_ref, g_ref, w_ref, ones_ref, cos_ref, slo_ref, shi_ref,
                   qg_ref, kg_ref, xg_ref, *out_and_scratch):
    q_ref, k_ref, v_ref = out_and_scratch[:3]
    dil, tl = q_ref.shape[1], q_ref.shape[3]
    if dil == 1:
        h_ref = out_and_scratch[-1]
        h_ref[...] = _rms_rows(x_ref[0], g_ref[...]).astype(BF16)
    else:
        stage_ref, h_ref = out_and_scratch[-2:]
        h = _rms_rows(x_ref[0], g_ref[...])
        for c in range(stage_ref.shape[0]):
            stage_ref[c] = h[:, c * LANES:(c + 1) * LANES]
        for r in range(dil):
            for c in range(stage_ref.shape[0]):
                h_ref[r * tl:(r + 1) * tl, c * LANES:(c + 1) * LANES] = (
                    stage_ref[c, pl.ds(r, tl, stride=dil), :].astype(BF16))

    def put(ref, p, y):
        for r in range(dil):
            ref[0, r, p] = y[r * tl:(r + 1) * tl].astype(BF16)

    ones = ones_ref[...]
    cos, slo, shi = cos_ref[...], slo_ref[...], shi_ref[...]
    pairs_per_chunk = MXU_COLS // LANES
    n_chunks = (nq + nk + nv + nx) // pairs_per_chunk

    def project(ch):
        return jnp.dot(h_ref[...], w_ref[:, ch * MXU_COLS:(ch + 1) * MXU_COLS],
                       preferred_element_type=F32)

    acc_next = project(0)
    for ch in range(n_chunks):
        acc = acc_next
        if ch + 1 < n_chunks:
            acc_next = project(ch + 1)
        ys = _lane_tiles(acc)
        first = ch * pairs_per_chunk
        if first < nq + nk or first >= nq + nk + nv:
            ssq = _lane_tiles(jnp.dot((acc * acc).astype(BF16), ones, preferred_element_type=F32))
        for half, y in enumerate(ys):
            p = first + half
            if p < nq + nk:
                ref, gain = (q_ref, qg_ref) if p < nq else (k_ref, kg_ref)
                p = p if p < nq else p - nq
                put(ref, p, _rope(_scale_rows(y, ssq[half], gain[p:p + 1, :]), cos, slo, shi))
            elif p < nq + nk + nv:
                put(v_ref, p - nq - nk, y)
            else:
                put(out_and_scratch[3], p - nq - nk - nv, _scale_rows(y, ssq[half], xg_ref[...]))


def _inproj(x, g, w, ones, rope, qg, kg, xg, nv, dil=1, with_xq=True):
    B, S, D = x.shape
    nq, nk, nx = qg.shape[0], kg.shape[0], (X_HEADS // 2 if with_xq else 0)
    tm = INPROJ_ROW_TILE
    tl = tm // dil
    rope = [t.reshape(S // tm, tl, dil, LANES).transpose(0, 2, 1, 3).reshape(S, LANES) for t in rope]
    tab = pl.BlockSpec((tm, LANES), lambda b, i: (i, 0))
    sizes = [nq, nk, nv] + ([nx] if with_xq else [])
    scratch = [pltpu.VMEM((tm, D), BF16)]
    if dil > 1:
        scratch.insert(0, pltpu.VMEM((D // LANES, tm, LANES), F32))
    return pl.pallas_call(
        functools.partial(_inproj_kernel, nq, nk, nv, nx),
        grid=(B, S // tm),
        in_specs=[pl.BlockSpec((1, tm, D), lambda b, i: (b, i, 0)), _full(g.shape), _full(w.shape),
                  _full(ones.shape), tab, tab, tab, _full(qg.shape), _full(kg.shape), _full(xg.shape)],
        out_specs=[pl.BlockSpec((1, dil, n, tl, LANES), lambda b, i: (b, 0, 0, i, 0)) for n in sizes],
        out_shape=[jax.ShapeDtypeStruct((B, dil, n, S // dil, LANES), BF16) for n in sizes],
        scratch_shapes=scratch,
        compiler_params=_params("parallel", "parallel"),
        name="inproj",
    )(x, g, w, ones, *rope, qg, kg, xg)


def _exp2_rows(s):
    e = jnp.exp2(s - jnp.max(s, axis=-1, keepdims=True))
    return e.astype(BF16), jnp.sum(e, axis=-1, keepdims=True)


def _diff_attn_kernel(lam_init, shifted, q_ref, k_ref, v_ref, lam_ref, sg_ref, o_ref, s_ref):
    seq = k_ref.shape[2]
    ring, _, rows, _ = s_ref.shape
    kc = KEY_CHUNK
    n_sub = q_ref.shape[2] // rows
    lp = lam_ref[...]
    lam = (jnp.exp(jnp.sum(lp[0:1] * lp[1:2], axis=-1, keepdims=True))
           - jnp.exp(jnp.sum(lp[2:3] * lp[3:4], axis=-1, keepdims=True)) + lam_init)
    wide = lambda col: jnp.broadcast_to(col, (rows, LANES))
    lag = 1 if shifted else 0
    row_max, coef = {}, {}
    for it in range(n_sub + lag + 1):
        sb_qk, sb_exp, sb_pv = it, it - lag, it - lag - 1
        do_qk, do_exp, do_pv = sb_qk < n_sub, 0 <= sb_exp < n_sub, 0 <= sb_pv
        if do_qk:
            qs = _split_heads(q_ref[0, 0, sb_qk * rows:(sb_qk + 1) * rows, :])
            m_wide = [None, None]
        if do_exp:
            l_wide = [None, None]
        acc = None
        for c0 in range(0, seq, kc):
            cols = slice(c0, c0 + kc)
            if do_qk:
                for h in range(2):
                    s = _scores(qs[h], k_ref[0, 0, cols, :])
                    if shifted:
                        s_ref[sb_qk % ring, h, :, cols] = s
                        m_wide[h] = functools.reduce(
                            jnp.maximum, _lane_tiles(s) + ([] if m_wide[h] is None else [m_wide[h]]))
                    else:
                        e = [jnp.exp2(t) for t in _lane_tiles(s)]
                        s_ref[sb_qk % ring, h, :, cols] = jnp.concatenate(e, axis=1)
                        l_wide[h] = functools.reduce(jnp.add, e + ([] if l_wide[h] is None else [l_wide[h]]))
            if shifted and do_exp:
                for h in range(2):
                    e = [jnp.exp2(t - row_max[sb_exp][h]) for t in _lane_tiles(s_ref[sb_exp % ring, h, :, cols])]
                    s_ref[sb_exp % ring, h, :, cols] = jnp.concatenate(e, axis=1)
                    l_wide[h] = functools.reduce(jnp.add, e + ([] if l_wide[h] is None else [l_wide[h]]))
            if do_pv:
                c1, c2 = coef[sb_pv]
                diff = [t1 * c1 - t2 * c2 for t1, t2 in zip(_lane_tiles(s_ref[sb_pv % ring, 0, :, cols]),
                                                           _lane_tiles(s_ref[sb_pv % ring, 1, :, cols]))]
                pv = jnp.dot(jnp.concatenate(diff, axis=1).astype(BF16), v_ref[0, 0, cols, :],
                             preferred_element_type=F32)
                acc = pv if acc is None else acc + pv
        if shifted and do_qk:
            row_max[sb_qk] = [wide(jnp.max(m, axis=-1, keepdims=True)) for m in m_wide]
        if do_exp:
            l1, l2 = [jnp.sum(l, axis=-1, keepdims=True) for l in l_wide]
            coef[sb_exp] = (wide(1.0 / l1), wide(lam / l2))
        if do_pv:
            o_ref[0, sb_pv * rows:(sb_pv + 1) * rows, :] = _rms_rows(acc, sg_ref[...]).astype(BF16)


def _diff_attn_call(shifted, lam_init, q, k, v, lam_p, subln):
    B, H, S, _ = q.shape
    tq = Q_TILE_A
    kv = pl.BlockSpec((1, 1, S, LANES), lambda b, h, i: (b, h, 0, 0))
    stages_in_flight = 3 if shifted else 2
    return pl.pallas_call(
        functools.partial(_diff_attn_kernel, lam_init, shifted),
        grid=(B, H, S // tq),
        in_specs=[pl.BlockSpec((1, 1, tq, LANES), lambda b, h, i: (b, h, i, 0)), kv, kv,
                  _full(lam_p.shape), _full(subln.shape)],
        out_specs=pl.BlockSpec((1, tq, LANES), lambda b, h, i: (b, i, h)),
        out_shape=jax.ShapeDtypeStruct((B, S, H * LANES), BF16),
        scratch_shapes=[pltpu.VMEM((stages_in_flight, 2, Q_SUB_A, S), F32)],
        compiler_params=_params("parallel", "parallel", "parallel"),
        name="diff_attn_shifted" if shifted else "diff_attn",
    )(q, k, v, lam_p, subln)


def _diff_attn(q, k, v, lam_p, subln, lam_init, score_bound):
    return lax.cond(score_bound <= SAFE_EXP2_RANGE,
                    functools.partial(_diff_attn_call, False, lam_init),
                    functools.partial(_diff_attn_call, True, lam_init),
                    q, k, v, lam_p, subln)


def _band_attn_kernel(radius, shifted, q_ref, k_ref, v_ref, o_ref, l_ref):
    _, res, pairs, rows, _ = q_ref.shape
    L = k_ref.shape[3]
    tq = min(Q_TILE_B, L)
    span = min(L, tq + 2 * radius)
    row0 = pl.program_id(2) * rows
    col = lax.broadcasted_iota(jnp.int32, (tq, span), 1)
    qrow = lax.broadcasted_iota(jnp.int32, (tq, span), 0)
    for r in range(res):
        for blk in range(rows // tq):
            q0 = row0 + blk * tq
            start = pl.multiple_of(jnp.clip(q0 - radius, 0, L - span), radius)
            valid = jnp.abs(col - qrow + (start - q0)) <= radius
            rs = slice(blk * tq, (blk + 1) * tq)
            scores = [[_scores(qh, k_ref[0, r, p, pl.ds(start, span), :])
                       for qh in _split_heads(q_ref[0, r, p, rs, :])] for p in range(pairs)]
            probs = []
            for p in range(pairs):
                per_head = []
                for s in scores[p]:
                    s = jnp.where(valid, s, NEG_INF)
                    m = jnp.max(s, axis=-1, keepdims=True) if shifted else 0.0
                    e = jnp.exp2(s - m) if shifted else jnp.exp2(s)
                    z = jnp.sum(e, axis=-1, keepdims=True)
                    per_head.append((e.astype(BF16), z, m + jnp.log2(z)))
                probs.append(per_head)
            for p in range(pairs):
                vw = v_ref[0, r, p, pl.ds(start, span), :]
                outs = [jnp.dot(e, vw, preferred_element_type=F32) / z for e, z, _ in probs[p]]
                lses = [jnp.broadcast_to(lse, (tq, LANES)) for _, _, lse in probs[p]]
                o_ref[0, r, p, rs, :] = _merge_heads(*outs).astype(BF16)
                l_ref[0, r, p, rs, :] = _merge_heads(*lses)


def _band_attn_call(radius, shifted, q, k, v):
    B, dil, pairs, L, _ = q.shape
    rows = min(L, ROW_TILE)
    res = ROW_TILE // rows
    kv = pl.BlockSpec((1, res, pairs, L, LANES), lambda b, r, i: (b, r, 0, 0, 0))
    tile = pl.BlockSpec((1, res, pairs, rows, LANES), lambda b, r, i: (b, r, 0, i, 0))
    return pl.pallas_call(
        functools.partial(_band_attn_kernel, radius, shifted),
        grid=(B, dil // res, L // rows),
        in_specs=[tile, kv, kv],
        out_specs=[tile, tile],
        out_shape=[jax.ShapeDtypeStruct(q.shape, BF16), jax.ShapeDtypeStruct(q.shape, F32)],
        compiler_params=_params("parallel", "parallel", "parallel"),
        name="band_attn_shifted" if shifted else "band_attn",
    )(q, k, v)


def _band_attn(q, k, v, radius, score_bound):
    return lax.cond(score_bound <= SAFE_EXP2_RANGE,
                    functools.partial(_band_attn_call, radius, False),
                    functools.partial(_band_attn_call, radius, True), q, k, v)


def _group_mix_into(o_refs, l_refs, stage_ref, dst_ref):
    def token_order(ref, p, slot):
        dil, tl = ref.shape[1], ref.shape[3]
        if dil == 1:
            return ref[0, 0, p].astype(F32)
        for r in range(dil):
            stage_ref[slot, pl.ds(r, tl, stride=dil), :] = ref[0, r, p].astype(F32)
        return stage_ref[slot]

    for p in range(o_refs[0].shape[2]):
        ls = [token_order(l, p, 2 * g) for g, l in enumerate(l_refs)]
        os_ = [token_order(o, p, 2 * g + 1) for g, o in enumerate(o_refs)]
        m = functools.reduce(jnp.maximum, ls)
        ws = [jnp.exp2(l - m) for l in ls]
        num = functools.reduce(jnp.add, [w * o for w, o in zip(ws, os_)])
        dst_ref[:, p * LANES:(p + 1) * LANES] = (num / functools.reduce(jnp.add, ws)).astype(BF16)


def _mem_kv_kernel(m_ref, g_ref, w_ref, ones_ref, kg_ref, k_ref, v_ref):
    h = _rms_rows(m_ref[0], g_ref[...]).astype(BF16)
    kv = jnp.dot(h, w_ref[...], preferred_element_type=F32)
    half = kv.shape[1] // 2
    for p in range(half // LANES):
        y = _pair_norm(kv[:, p * LANES:(p + 1) * LANES], ones_ref[...], kg_ref[...])
        k_ref[0, :, p * LANES:(p + 1) * LANES] = y.astype(BF16)
    v_ref[0] = kv[:, half:].astype(BF16)


def _mem_kv(mem, g, w, ones, kg):
    B, M, D = mem.shape
    width = w.shape[1] // 2
    blk = pl.BlockSpec((1, M, width), lambda b: (b, 0, 0))
    return pl.pallas_call(
        _mem_kv_kernel,
        grid=(B,),
        in_specs=[pl.BlockSpec((1, M, D), lambda b: (b, 0, 0)), _full(g.shape), _full(w.shape),
                  _full(ones.shape), _full(kg.shape)],
        out_specs=[blk, blk],
        out_shape=[jax.ShapeDtypeStruct((B, M, width), BF16)] * 2,
        compiler_params=_params("parallel"),
        name="mem_kv",
    )(mem, g, w, ones, kg)


def _cross_attn_into(q_ref, k_ref, v_ref, dst_ref, col0):
    pairs = range(q_ref.shape[2])
    cols = [slice(p * LANES, (p + 1) * LANES) for p in pairs]
    scores = [[_scores(qh, k_ref[0, :, cols[p]]) for qh in _split_heads(q_ref[0, 0, p])] for p in pairs]
    probs = [[_exp2_rows(s) for s in scores[p]] for p in pairs]
    for p in pairs:
        v = v_ref[0, :, cols[p]]
        outs = [jnp.dot(e, v, preferred_element_type=F32) / l for e, l in probs[p]]
        dst_ref[:, col0 + p * LANES:col0 + (p + 1) * LANES] = _merge_heads(*outs).astype(BF16)


def _outproj_kernel(n_groups, *refs):
    if n_groups:
        o_refs, l_refs = refs[:n_groups], refs[n_groups:2 * n_groups]
        xq_ref, mk_ref, mv_ref, x_ref, w_ref, out_ref, lhs_ref, stage_ref = refs[2 * n_groups:]
        mix_width = o_refs[0].shape[2] * LANES
        _group_mix_into(o_refs, l_refs, stage_ref, lhs_ref)
        _cross_attn_into(xq_ref, mk_ref, mv_ref, lhs_ref, mix_width)
        out_ref[0] = x_ref[0] + jnp.dot(lhs_ref[...], w_ref[...], preferred_element_type=F32)
    else:
        mix_ref, xq_ref, mk_ref, mv_ref, x_ref, w_ref, out_ref, lhs_ref = refs
        mix_width = mix_ref.shape[2]
        _cross_attn_into(xq_ref, mk_ref, mv_ref, lhs_ref, 0)
        out_ref[0] = (x_ref[0]
                      + jnp.dot(mix_ref[0], w_ref[:mix_width, :], preferred_element_type=F32)
                      + jnp.dot(lhs_ref[...], w_ref[mix_width:, :], preferred_element_type=F32))


def _outproj(mixed, xq, mk, mv, x, w):
    B, S, D = x.shape
    M, cross_width = mk.shape[1:]
    tm = OUTPROJ_ROW_TILE
    row = lambda c: pl.BlockSpec((1, tm, c), lambda b, i: (b, i, 0))
    kv = pl.BlockSpec((1, M, cross_width), lambda b, i: (b, 0, 0))
    common = [pl.BlockSpec((1, 1, xq.shape[2], tm, LANES), lambda b, i: (b, 0, 0, i, 0)), kv, kv,
              row(D), _full(w.shape)]
    if isinstance(mixed, tuple):
        groups = [*mixed[0], *mixed[1]]
        pairs = groups[0].shape[2]
        group_spec = lambda t: pl.BlockSpec((1, t.shape[1], pairs, tm // t.shape[1], LANES),
                                            lambda b, i: (b, 0, 0, i, 0))
        in_specs = [group_spec(t) for t in groups] + common
        args = groups
        scratch = [pltpu.VMEM((tm, pairs * LANES + cross_width), BF16),
                   pltpu.VMEM((len(groups), tm, LANES), F32)]
        n_groups = len(mixed[0])
    else:
        in_specs = [row(mixed.shape[2])] + common
        args = [mixed]
        scratch = [pltpu.VMEM((tm, cross_width), BF16)]
        n_groups = 0
    return pl.pallas_call(
        functools.partial(_outproj_kernel, n_groups),
        grid=(B, S // tm),
        in_specs=in_specs,
        out_specs=row(D),
        out_shape=jax.ShapeDtypeStruct((B, S, D), F32),
        scratch_shapes=scratch,
        compiler_params=_params("parallel", "parallel"),
        name="outproj",
    )(*args, xq, mk, mv, x, w)


FF_CHUNK = 256


HALO_ROWS = 16


def _ffn_kernel(x_ref, prev_ref, next_ref, g_ref, wu_ref, cw_ref, cb_ref, wd_ref, o_ref,
                h_ref, act_ref):
    tm = x_ref.shape[1]
    i, n = pl.program_id(1), pl.num_programs(1)
    g = g_ref[...]
    h_ref[0:HALO_ROWS] = (_rms_rows(prev_ref[0], g) * (i > 0).astype(F32)).astype(BF16)
    h_ref[HALO_ROWS:HALO_ROWS + tm] = _rms_rows(x_ref[0], g).astype(BF16)
    h_ref[HALO_ROWS + tm:] = (_rms_rows(next_ref[0], g) * (i < n - 1).astype(F32)).astype(BF16)
    main = slice(HALO_ROWS, HALO_ROWS + tm)

    def project(c0):
        return [jnp.dot(h_ref[...], wu_ref[:, col:col + FF_CHUNK], preferred_element_type=F32)
                for col in (c0, D_FF + c0)]

    def conv(u, col):
        cols = slice(col, col + FF_CHUNK)
        before = pltpu.roll(u, 1, axis=0)[main]
        after = pltpu.roll(u, u.shape[0] - 1, axis=0)[main]
        return (before * cw_ref[0:1, cols] + u[main] * cw_ref[1:2, cols] + after * cw_ref[2:3, cols]
                + cb_ref[:, cols])

    starts = list(range(0, D_FF, FF_CHUNK))
    u_next = project(starts[0])
    for idx, c0 in enumerate(starts):
        u_a, u_b = u_next
        if idx + 1 < len(starts):
            u_next = project(starts[idx + 1])
        a, b = conv(u_a, c0), conv(u_b, D_FF + c0)
        act_ref[:, c0:c0 + FF_CHUNK] = (a / (1.0 + jnp.exp(-a)) * b).astype(BF16)
    o_ref[0] = x_ref[0] + jnp.dot(act_ref[...], wd_ref[...], preferred_element_type=F32)


def _ffn(x, g, w_up, conv_w, conv_b, w_down):
    B, S, D = x.shape
    tm = FFN_ROW_TILE
    halo_per_tile = tm // HALO_ROWS
    last_halo = S // HALO_ROWS - 1
    tile = pl.BlockSpec((1, tm, D), lambda b, i: (b, i, 0))
    return pl.pallas_call(
        _ffn_kernel,
        grid=(B, S // tm),
        in_specs=[tile,
                  pl.BlockSpec((1, HALO_ROWS, D), lambda b, i: (b, jnp.maximum(i * halo_per_tile - 1, 0), 0)),
                  pl.BlockSpec((1, HALO_ROWS, D),
                               lambda b, i: (b, jnp.minimum((i + 1) * halo_per_tile, last_halo), 0)),
                  _full(g.shape), _resident(w_up.shape), _full(conv_w.shape), _full(conv_b.shape),
                  _resident(w_down.shape)],
        out_specs=tile,
        out_shape=jax.ShapeDtypeStruct((B, S, D), F32),
        scratch_shapes=[pltpu.VMEM((tm + 2 * HALO_ROWS, D), BF16), pltpu.VMEM((tm, D_FF), BF16)],
        compiler_params=_params("parallel", "parallel"),
        name="ffn",
    )(x, x, x, g, w_up, conv_w, conv_b, w_down)


def _rope_tables(S):
    inv = ROPE_THETA ** (-(jnp.arange(ROT_HALF, dtype=F32) * 2.0 / (2 * ROT_HALF)))
    ang = jnp.arange(S, dtype=F32)[:, None] * inv[None, :]
    cos, sin = jnp.cos(ang), jnp.sin(ang)
    rest = HEAD_DIM - 2 * ROT_HALF
    one, zero = jnp.ones((S, rest), F32), jnp.zeros((S, rest), F32)
    zhalf = jnp.zeros((S, ROT_HALF), F32)
    per_head = (jnp.concatenate([cos, cos, one], axis=1),
                jnp.concatenate([zhalf, sin, zero], axis=1),
                jnp.concatenate([-sin, zhalf, zero], axis=1))
    return tuple(jnp.tile(t, (1, LANES // HEAD_DIM)) for t in per_head)


def _pair_gain(g, scale=1.0):
    g = jnp.atleast_2d(g.astype(F32) * scale)
    return jnp.tile(g, (1, LANES // HEAD_DIM))


def _group_ones(width):
    head = jnp.arange(width) // HEAD_DIM
    return (head[:, None] == head[None, :]).astype(BF16)


def _score_bound(q_gain, k_gain):
    return (NORM_SLACK * HEAD_DIM * Q_SCALE * jnp.max(jnp.abs(q_gain.astype(F32)))
            * jnp.max(jnp.abs(k_gain.astype(F32))))


def _a_weight(w):
    d = w.shape[0]
    n = A_HEADS * HEAD_DIM
    pair = lambda t: t.reshape(d, 2, A_HEADS, HEAD_DIM).transpose(0, 2, 1, 3).reshape(d, 2 * n)
    return jnp.concatenate([pair(w[:, :2 * n]), pair(w[:, 2 * n:4 * n]), w[:, 4 * n:]], axis=1).astype(BF16)


def _trunk(x, mem, P):
    S = x.shape[1]
    rope = _rope_tables(S)
    ones = _group_ones(MXU_COLS)
    n_layers = P["norm_mix"].shape[0]
    for i in range(n_layers):
        j = i // 2
        row = lambda name: P[name][i][None, :].astype(F32)
        mk, mv = _mem_kv(mem, row("norm_mem"), P["w_mem_kv"][i].astype(BF16), _group_ones(LANES),
                         _pair_gain(P["xk_norm"][i]))
        xg = _pair_gain(P["xq_norm"][i], Q_SCALE)
        if i % 2 == 0:
            qg = jnp.tile(_pair_gain(P["a_q_norm"][j], Q_SCALE), (A_HEADS, 1))
            kg = jnp.tile(_pair_gain(P["a_k_norm"][j]), (A_HEADS, 1))
            q, k, v, xq = _inproj(x, row("norm_mix"), _a_weight(P["a_w_in"][j]), ones, rope,
                                  qg, kg, xg, A_HEADS)
            lam_init = 0.8 - 0.6 * math.exp(-0.3 * i)
            subln = P["a_subln"][j][None, :].astype(F32) * (1.0 - lam_init)
            mixed = _diff_attn(q[:, 0], k[:, 0], v[:, 0], P["a_lambda"][j].astype(F32), subln, lam_init,
                               _score_bound(P["a_q_norm"][j], P["a_k_norm"][j]))
            w_out = P["a_w_out"][j]
        else:
            pairs = B_HEADS // 2
            width = B_HEADS * HEAD_DIM
            w_in = P["b_w_in"][j]
            outs, lses = [], []
            for gi, (window, dil) in enumerate(B_GROUPS):
                cols = [w_in[:, s * len(B_GROUPS) * width + gi * width:][:, :width] for s in range(3)]
                if gi == 0:
                    cols.append(w_in[:, 3 * len(B_GROUPS) * width:])
                qg = jnp.tile(_pair_gain(P["b_q_norm"][j][gi], Q_SCALE), (pairs, 1))
                kg = jnp.tile(_pair_gain(P["b_k_norm"][j][gi]), (pairs, 1))
                q, k, v, *rest = _inproj(x, row("norm_mix"), jnp.concatenate(cols, axis=1).astype(BF16),
                                         ones, rope, qg, kg, xg, pairs, dil=dil, with_xq=(gi == 0))
                if gi == 0:
                    xq = rest[0]
                o, l = _band_attn(q, k, v, (window // 2) // dil,
                                  _score_bound(P["b_q_norm"][j][gi], P["b_k_norm"][j][gi]))
                outs.append(o)
                lses.append(l)
            mixed = (outs, lses)
            w_out = P["b_w_out"][j]
        x = _outproj(mixed, xq, mk, mv, x, w_out.astype(BF16))
        x = _ffn(x, row("norm_ffn"), P["w_up"][i].astype(BF16), P["conv_w"][i].astype(F32),
                 P["conv_b"][i][None, :].astype(F32), P["w_down"][i].astype(BF16))
    return x


def kernel(x_prompt, x_sample, mem_prompt, mem_sample, norm_mix, norm_mem, w_mem_kv, xq_norm, xk_norm, a_w_in, a_w_out, a_q_norm, a_k_norm, a_lambda, a_subln, b_w_in, b_w_out, b_q_norm, b_k_norm, norm_ffn, w_up, conv_w, conv_b, w_down):
    P = dict(norm_mix=norm_mix, norm_mem=norm_mem, w_mem_kv=w_mem_kv, xq_norm=xq_norm, xk_norm=xk_norm,
             a_w_in=a_w_in, a_w_out=a_w_out, a_q_norm=a_q_norm, a_k_norm=a_k_norm, a_lambda=a_lambda,
             a_subln=a_subln, b_w_in=b_w_in, b_w_out=b_w_out, b_q_norm=b_q_norm, b_k_norm=b_k_norm,
             norm_ffn=norm_ffn, w_up=w_up, conv_w=conv_w, conv_b=conv_b, w_down=w_down)
    return _trunk(x_prompt, mem_prompt, P), _trunk(x_sample, mem_sample, P)
```

```python
import functools
import math

import jax
import jax.numpy as jnp
from jax import lax
from jax.experimental import pallas as pl
from jax.experimental.pallas import tpu as pltpu

F32 = jnp.float32
BF16 = jnp.bfloat16

LANES = 128
HEAD_DIM = 64
ROT_HALF = 8
A_HEADS = 8
B_HEADS = 8
B_GROUPS = ((128, 1), (512, 4), (2048, 16))
X_HEADS = 4
D_FF = 2816
ROPE_THETA = 500000.0
EPS = 1e-6
NEG_INF = -1e30
LOG2E = math.log2(math.e)
Q_SCALE = LOG2E / math.sqrt(HEAD_DIM)

V7X_VMEM_BYTES = 64 * 1024 * 1024
VMEM_LIMIT = V7X_VMEM_BYTES * 7 // 8

ROW_TILE = 2048
INPROJ_ROW_TILE = 1024
OUTPROJ_ROW_TILE = 1024
FFN_ROW_TILE = 1024
Q_TILE_A = 2048
HEADS_PER_STEP_A = 2
Q_SUB_A = 256
KEY_CHUNK = 512
SAFE_EXP2_RANGE = 50.0
NORM_SLACK = 1.02
Q_TILE_B = 128
MXU_COLS = 256


def _params(*sem):
    return pltpu.CompilerParams(dimension_semantics=sem, vmem_limit_bytes=VMEM_LIMIT)


def _full(shape):
    return pl.BlockSpec(shape, lambda *_: (0,) * len(shape))


def _resident(shape):
    return pl.BlockSpec(shape, lambda *_: (0,) * len(shape), pipeline_mode=pl.Buffered(1))


def _rms_rows(x, g):
    ms = jnp.mean(x * x, axis=-1, keepdims=True)
    return x * lax.rsqrt(ms + EPS) * g


def _lane_tiles(x):
    return [x[:, t * LANES:(t + 1) * LANES] for t in range(x.shape[1] // LANES)]


def _scale_rows(y, ssq, gain):
    return y * lax.rsqrt(ssq * (1.0 / HEAD_DIM) + EPS) * gain


def _pair_norm(y, group_ones, gain):
    ssq = jnp.dot((y * y).astype(BF16), group_ones, preferred_element_type=F32)
    return _scale_rows(y, ssq, gain)


def _rope(y, cos, sin_lo, sin_hi):
    return (y * cos + pltpu.roll(y, ROT_HALF, axis=1) * sin_lo
            + pltpu.roll(y, LANES - ROT_HALF, axis=1) * sin_hi)


def _split_heads(q):
    lane = lax.broadcasted_iota(jnp.int32, q.shape, 1)
    zero = jnp.zeros_like(q)
    return jnp.where(lane < HEAD_DIM, q, zero), jnp.where(lane >= HEAD_DIM, q, zero)


def _merge_heads(o_lo, o_hi):
    lane = lax.broadcasted_iota(jnp.int32, o_lo.shape, 1)
    return jnp.where(lane < HEAD_DIM, o_lo, o_hi)


def _scores(q, k):
    return lax.dot_general(q, k, (((1,), (1,)), ((), ())), preferred_element_type=F32)


def _inproj_kernel(nq, nk, nv, nx, x_ref, g_ref, w_ref, ones_ref, cos_ref, slo_ref, shi_ref,
                   qg_ref, kg_ref, xg_ref, *out_and_scratch):
    q_ref, k_ref, v_ref = out_and_scratch[:3]
    dil, tl = q_ref.shape[1], q_ref.shape[3]
    if dil == 1:
        h_ref = out_and_scratch[-1]
        h_ref[...] = _rms_rows(x_ref[0], g_ref[...]).astype(BF16)
    else:
        stage_ref, h_ref = out_and_scratch[-2:]
        h = _rms_rows(x_ref[0], g_ref[...])
        for c in range(stage_ref.shape[0]):
            stage_ref[c] = h[:, c * LANES:(c + 1) * LANES]
        for r in range(dil):
            for c in range(stage_ref.shape[0]):
                h_ref[r * tl:(r + 1) * tl, c * LANES:(c + 1) * LANES] = (
                    stage_ref[c, pl.ds(r, tl, stride=dil), :].astype(BF16))

    def put(ref, p, y):
        for r in range(dil):
            ref[0, r, p] = y[r * tl:(r + 1) * tl].astype(BF16)

    ones = ones_ref[...]
    cos, slo, shi = cos_ref[...], slo_ref[...], shi_ref[...]
    pairs_per_chunk = MXU_COLS // LANES
    n_chunks = (nq + nk + nv + nx) // pairs_per_chunk

    def project(ch):
        return jnp.dot(h_ref[...], w_ref[:, ch * MXU_COLS:(ch + 1) * MXU_COLS],
                       preferred_element_type=F32)

    acc_next = project(0)
    for ch in range(n_chunks):
        acc = acc_next
        if ch + 1 < n_chunks:
            acc_next = project(ch + 1)
        ys = _lane_tiles(acc)
        first = ch * pairs_per_chunk
        if first < nq + nk or first >= nq + nk + nv:
            ssq = _lane_tiles(jnp.dot((acc * acc).astype(BF16), ones, preferred_element_type=F32))
        for half, y in enumerate(ys):
            p = first + half
            if p < nq + nk:
                ref, gain = (q_ref, qg_ref) if p < nq else (k_ref, kg_ref)
                p = p if p < nq else p - nq
                put(ref, p, _rope(_scale_rows(y, ssq[half], gain[p:p + 1, :]), cos, slo, shi))
            elif p < nq + nk + nv:
                put(v_ref, p - nq - nk, y)
            else:
                put(out_and_scratch[3], p - nq - nk - nv, _scale_rows(y, ssq[half], xg_ref[...]))


def _inproj(x, g, w, ones, rope, qg, kg, xg, nv, dil=1, with_xq=True):
    B, S, D = x.shape
    nq, nk, nx = qg.shape[0], kg.shape[0], (X_HEADS // 2 if with_xq else 0)
    tm = INPROJ_ROW_TILE
    tl = tm // dil
    rope = [t.reshape(S // tm, tl, dil, LANES).transpose(0, 2, 1, 3).reshape(S, LANES) for t in rope]
    tab = pl.BlockSpec((tm, LANES), lambda b, i: (i, 0))
    sizes = [nq, nk, nv] + ([nx] if with_xq else [])
    scratch = [pltpu.VMEM((tm, D), BF16)]
    if dil > 1:
        scratch.insert(0, pltpu.VMEM((D // LANES, tm, LANES), F32))
    return pl.pallas_call(
        functools.partial(_inproj_kernel, nq, nk, nv, nx),
        grid=(B, S // tm),
        in_specs=[pl.BlockSpec((1, tm, D), lambda b, i: (b, i, 0)), _full(g.shape), _full(w.shape),
                  _full(ones.shape), tab, tab, tab, _full(qg.shape), _full(kg.shape), _full(xg.shape)],
        out_specs=[pl.BlockSpec((1, dil, n, tl, LANES), lambda b, i: (b, 0, 0, i, 0)) for n in sizes],
        out_shape=[jax.ShapeDtypeStruct((B, dil, n, S // dil, LANES), BF16) for n in sizes],
        scratch_shapes=scratch,
        compiler_params=_params("parallel", "parallel"),
        name="inproj",
    )(x, g, w, ones, *rope, qg, kg, xg)


def _exp2_rows(s):
    e = jnp.exp2(s - jnp.max(s, axis=-1, keepdims=True))
    return e.astype(BF16), jnp.sum(e, axis=-1, keepdims=True)


def _diff_attn_kernel(lam_init, shifted, q_ref, k_ref, v_ref, lam_ref, sg_ref, o_ref, s_ref):
    seq = k_ref.shape[2]
    ring, _, rows, _ = s_ref.shape
    kc = KEY_CHUNK
    per_head = q_ref.shape[2] // rows
    n_sub = q_ref.shape[1] * per_head
    head_of = lambda sb: sb // per_head
    rows_of = lambda sb: slice((sb % per_head) * rows, (sb % per_head + 1) * rows)
    lp = lam_ref[...]
    lam = (jnp.exp(jnp.sum(lp[0:1] * lp[1:2], axis=-1, keepdims=True))
           - jnp.exp(jnp.sum(lp[2:3] * lp[3:4], axis=-1, keepdims=True)) + lam_init)
    wide = lambda col: jnp.broadcast_to(col, (rows, LANES))
    lag = 1 if shifted else 0
    row_max, coef = {}, {}
    for it in range(n_sub + lag + 1):
        sb_qk, sb_exp, sb_pv = it, it - lag, it - lag - 1
        do_qk, do_exp, do_pv = sb_qk < n_sub, 0 <= sb_exp < n_sub, 0 <= sb_pv
        if do_qk:
            qs = _split_heads(q_ref[0, head_of(sb_qk), rows_of(sb_qk), :])
            m_wide = [None, None]
        if do_exp:
            l_wide = [None, None]
        acc = None
        for c0 in range(0, seq, kc):
            cols = slice(c0, c0 + kc)
            if do_qk:
                for h in range(2):
                    s = _scores(qs[h], k_ref[0, head_of(sb_qk), cols, :])
                    if shifted:
                        s_ref[sb_qk % ring, h, :, cols] = s
                        m_wide[h] = functools.reduce(
                            jnp.maximum, _lane_tiles(s) + ([] if m_wide[h] is None else [m_wide[h]]))
                    else:
                        e = [jnp.exp2(t) for t in _lane_tiles(s)]
                        s_ref[sb_qk % ring, h, :, cols] = jnp.concatenate(e, axis=1)
                        l_wide[h] = functools.reduce(jnp.add, e + ([] if l_wide[h] is None else [l_wide[h]]))
            if shifted and do_exp:
                for h in range(2):
                    e = [jnp.exp2(t - row_max[sb_exp][h]) for t in _lane_tiles(s_ref[sb_exp % ring, h, :, cols])]
                    s_ref[sb_exp % ring, h, :, cols] = jnp.concatenate(e, axis=1)
                    l_wide[h] = functools.reduce(jnp.add, e + ([] if l_wide[h] is None else [l_wide[h]]))
            if do_pv:
                c1, c2 = coef[sb_pv]
                diff = [t1 * c1 - t2 * c2 for t1, t2 in zip(_lane_tiles(s_ref[sb_pv % ring, 0, :, cols]),
                                                           _lane_tiles(s_ref[sb_pv % ring, 1, :, cols]))]
                pv = jnp.dot(jnp.concatenate(diff, axis=1).astype(BF16), v_ref[0, head_of(sb_pv), cols, :],
                             preferred_element_type=F32)
                acc = pv if acc is None else acc + pv
        if shifted and do_qk:
            row_max[sb_qk] = [wide(jnp.max(m, axis=-1, keepdims=True)) for m in m_wide]
        if do_exp:
            l1, l2 = [jnp.sum(l, axis=-1, keepdims=True) for l in l_wide]
            coef[sb_exp] = (wide(1.0 / l1), wide(lam / l2))
        if do_pv:
            hd = head_of(sb_pv)
            o_ref[0, rows_of(sb_pv), hd * LANES:(hd + 1) * LANES] = _rms_rows(acc, sg_ref[...]).astype(BF16)


def _diff_attn_call(shifted, lam_init, q, k, v, lam_p, subln):
    B, H, S, _ = q.shape
    tq, hs = Q_TILE_A, HEADS_PER_STEP_A
    kv = pl.BlockSpec((1, hs, S, LANES), lambda b, h, i: (b, h, 0, 0))
    stages_in_flight = 3 if shifted else 2
    return pl.pallas_call(
        functools.partial(_diff_attn_kernel, lam_init, shifted),
        grid=(B, H // hs, S // tq),
        in_specs=[pl.BlockSpec((1, hs, tq, LANES), lambda b, h, i: (b, h, i, 0)), kv, kv,
                  _full(lam_p.shape), _full(subln.shape)],
        out_specs=pl.BlockSpec((1, tq, hs * LANES), lambda b, h, i: (b, i, h)),
        out_shape=jax.ShapeDtypeStruct((B, S, H * LANES), BF16),
        scratch_shapes=[pltpu.VMEM((stages_in_flight, 2, Q_SUB_A, S), F32)],
        compiler_params=_params("parallel", "parallel", "parallel"),
        name="diff_attn_shifted" if shifted else "diff_attn",
    )(q, k, v, lam_p, subln)


def _diff_attn(q, k, v, lam_p, subln, lam_init, score_bound):
    return lax.cond(score_bound <= SAFE_EXP2_RANGE,
                    functools.partial(_diff_attn_call, False, lam_init),
                    functools.partial(_diff_attn_call, True, lam_init),
                    q, k, v, lam_p, subln)


def _band_attn_kernel(radius, shifted, q_ref, k_ref, v_ref, o_ref, l_ref):
    _, res, pairs, rows, _ = q_ref.shape
    L = k_ref.shape[3]
    tq = min(Q_TILE_B, L)
    span = min(L, tq + 2 * radius)
    row0 = pl.program_id(2) * rows
    col = lax.broadcasted_iota(jnp.int32, (tq, span), 1)
    qrow = lax.broadcasted_iota(jnp.int32, (tq, span), 0)
    for r in range(res):
        for blk in range(rows // tq):
            q0 = row0 + blk * tq
            start = pl.multiple_of(jnp.clip(q0 - radius, 0, L - span), radius)
            valid = jnp.abs(col - qrow + (start - q0)) <= radius
            rs = slice(blk * tq, (blk + 1) * tq)
            scores = [[_scores(qh, k_ref[0, r, p, pl.ds(start, span), :])
                       for qh in _split_heads(q_ref[0, r, p, rs, :])] for p in range(pairs)]
            probs = []
            for p in range(pairs):
                per_head = []
                for s in scores[p]:
                    s = jnp.where(valid, s, NEG_INF)
                    m = jnp.max(s, axis=-1, keepdims=True) if shifted else 0.0
                    e = jnp.exp2(s - m) if shifted else jnp.exp2(s)
                    z = jnp.sum(e, axis=-1, keepdims=True)
                    per_head.append((e.astype(BF16), z, m + jnp.log2(z)))
                probs.append(per_head)
            for p in range(pairs):
                vw = v_ref[0, r, p, pl.ds(start, span), :]
                outs = [jnp.dot(e, vw, preferred_element_type=F32) / z for e, z, _ in probs[p]]
                lses = [jnp.broadcast_to(lse, (tq, LANES)) for _, _, lse in probs[p]]
                o_ref[0, r, p, rs, :] = _merge_heads(*outs).astype(BF16)
                l_ref[0, r, p, rs, :] = _merge_heads(*lses)


def _band_attn_call(radius, shifted, q, k, v):
    B, dil, pairs, L, _ = q.shape
    rows = min(L, ROW_TILE)
    res = ROW_TILE // rows
    kv = pl.BlockSpec((1, res, pairs, L, LANES), lambda b, r, i: (b, r, 0, 0, 0))
    tile = pl.BlockSpec((1, res, pairs, rows, LANES), lambda b, r, i: (b, r, 0, i, 0))
    return pl.pallas_call(
        functools.partial(_band_attn_kernel, radius, shifted),
        grid=(B, dil // res, L // rows),
        in_specs=[tile, kv, kv],
        out_specs=[tile, tile],
        out_shape=[jax.ShapeDtypeStruct(q.shape, BF16), jax.ShapeDtypeStruct(q.shape, F32)],
        compiler_params=_params("parallel", "parallel", "parallel"),
        name="band_attn_shifted" if shifted else "band_attn",
    )(q, k, v)


def _band_attn(q, k, v, radius, score_bound):
    return lax.cond(score_bound <= SAFE_EXP2_RANGE,
                    functools.partial(_band_attn_call, radius, False),
                    functools.partial(_band_attn_call, radius, True), q, k, v)


def _group_mix_into(o_refs, l_refs, stage_ref, dst_ref):
    def token_order(ref, p, slot):
        dil, tl = ref.shape[1], ref.shape[3]
        if dil == 1:
            return ref[0, 0, p].astype(F32)
        for r in range(dil):
            stage_ref[slot, pl.ds(r, tl, stride=dil), :] = ref[0, r, p].astype(F32)
        return stage_ref[slot]

    for p in range(o_refs[0].shape[2]):
        ls = [token_order(l, p, 2 * g) for g, l in enumerate(l_refs)]
        os_ = [token_order(o, p, 2 * g + 1) for g, o in enumerate(o_refs)]
        m = functools.reduce(jnp.maximum, ls)
        ws = [jnp.exp2(l - m) for l in ls]
        num = functools.reduce(jnp.add, [w * o for w, o in zip(ws, os_)])
        dst_ref[:, p * LANES:(p + 1) * LANES] = (num / functools.reduce(jnp.add, ws)).astype(BF16)


def _mem_kv_kernel(m_ref, g_ref, w_ref, ones_ref, kg_ref, k_ref, v_ref):
    h = _rms_rows(m_ref[0], g_ref[...]).astype(BF16)
    kv = jnp.dot(h, w_ref[...], preferred_element_type=F32)
    half = kv.shape[1] // 2
    for p in range(half // LANES):
        y = _pair_norm(kv[:, p * LANES:(p + 1) * LANES], ones_ref[...], kg_ref[...])
        k_ref[0, :, p * LANES:(p + 1) * LANES] = y.astype(BF16)
    v_ref[0] = kv[:, half:].astype(BF16)


def _mem_kv(mem, g, w, ones, kg):
    B, M, D = mem.shape
    width = w.shape[1] // 2
    blk = pl.BlockSpec((1, M, width), lambda b: (b, 0, 0))
    return pl.pallas_call(
        _mem_kv_kernel,
        grid=(B,),
        in_specs=[pl.BlockSpec((1, M, D), lambda b: (b, 0, 0)), _full(g.shape), _full(w.shape),
                  _full(ones.shape), _full(kg.shape)],
        out_specs=[blk, blk],
        out_shape=[jax.ShapeDtypeStruct((B, M, width), BF16)] * 2,
        compiler_params=_params("parallel"),
        name="mem_kv",
    )(mem, g, w, ones, kg)


def _cross_attn_into(q_ref, k_ref, v_ref, dst_ref, col0):
    pairs = range(q_ref.shape[2])
    cols = [slice(p * LANES, (p + 1) * LANES) for p in pairs]
    scores = [[_scores(qh, k_ref[0, :, cols[p]]) for qh in _split_heads(q_ref[0, 0, p])] for p in pairs]
    probs = [[_exp2_rows(s) for s in scores[p]] for p in pairs]
    for p in pairs:
        v = v_ref[0, :, cols[p]]
        outs = [jnp.dot(e, v, preferred_element_type=F32) / l for e, l in probs[p]]
        dst_ref[:, col0 + p * LANES:col0 + (p + 1) * LANES] = _merge_heads(*outs).astype(BF16)


def _outproj_kernel(n_groups, *refs):
    if n_groups:
        o_refs, l_refs = refs[:n_groups], refs[n_groups:2 * n_groups]
        xq_ref, mk_ref, mv_ref, x_ref, w_ref, out_ref, lhs_ref, stage_ref = refs[2 * n_groups:]
        mix_width = o_refs[0].shape[2] * LANES
        _group_mix_into(o_refs, l_refs, stage_ref, lhs_ref)
        _cross_attn_into(xq_ref, mk_ref, mv_ref, lhs_ref, mix_width)
        out_ref[0] = x_ref[0] + jnp.dot(lhs_ref[...], w_ref[...], preferred_element_type=F32)
    else:
        mix_ref, xq_ref, mk_ref, mv_ref, x_ref, w_ref, out_ref, lhs_ref = refs
        mix_width = mix_ref.shape[2]
        _cross_attn_into(xq_ref, mk_ref, mv_ref, lhs_ref, 0)
        out_ref[0] = (x_ref[0]
                      + jnp.dot(mix_ref[0], w_ref[:mix_width, :], preferred_element_type=F32)
                      + jnp.dot(lhs_ref[...], w_ref[mix_width:, :], preferred_element_type=F32))


def _outproj(mixed, xq, mk, mv, x, w):
    B, S, D = x.shape
    M, cross_width = mk.shape[1:]
    tm = OUTPROJ_ROW_TILE
    row = lambda c: pl.BlockSpec((1, tm, c), lambda b, i: (b, i, 0))
    kv = pl.BlockSpec((1, M, cross_width), lambda b, i: (b, 0, 0))
    common = [pl.BlockSpec((1, 1, xq.shape[2], tm, LANES), lambda b, i: (b, 0, 0, i, 0)), kv, kv,
              row(D), _full(w.shape)]
    if isinstance(mixed, tuple):
        groups = [*mixed[0], *mixed[1]]
        pairs = groups[0].shape[2]
        group_spec = lambda t: pl.BlockSpec((1, t.shape[1], pairs, tm // t.shape[1], LANES),
                                            lambda b, i: (b, 0, 0, i, 0))
        in_specs = [group_spec(t) for t in groups] + common
        args = groups
        scratch = [pltpu.VMEM((tm, pairs * LANES + cross_width), BF16),
                   pltpu.VMEM((len(groups), tm, LANES), F32)]
        n_groups = len(mixed[0])
    else:
        in_specs = [row(mixed.shape[2])] + common
        args = [mixed]
        scratch = [pltpu.VMEM((tm, cross_width), BF16)]
        n_groups = 0
    return pl.pallas_call(
        functools.partial(_outproj_kernel, n_groups),
        grid=(B, S // tm),
        in_specs=in_specs,
        out_specs=row(D),
        out_shape=jax.ShapeDtypeStruct((B, S, D), F32),
        scratch_shapes=scratch,
        compiler_params=_params("parallel", "parallel"),
        name="outproj",
    )(*args, xq, mk, mv, x, w)


FF_CHUNK = 256


HALO_ROWS = 16


def _ffn_kernel(x_ref, prev_ref, next_ref, g_ref, wu_ref, cw_ref, cb_ref, wd_ref, o_ref,
                h_ref, act_ref):
    tm = x_ref.shape[1]
    i, n = pl.program_id(1), pl.num_programs(1)
    g = g_ref[...]
    h_ref[0:HALO_ROWS] = (_rms_rows(prev_ref[0], g) * (i > 0).astype(F32)).astype(BF16)
    h_ref[HALO_ROWS:HALO_ROWS + tm] = _rms_rows(x_ref[0], g).astype(BF16)
    h_ref[HALO_ROWS + tm:] = (_rms_rows(next_ref[0], g) * (i < n - 1).astype(F32)).astype(BF16)
    main = slice(HALO_ROWS, HALO_ROWS + tm)

    def project(c0):
        return [jnp.dot(h_ref[...], wu_ref[:, col:col + FF_CHUNK], preferred_element_type=F32)
                for col in (c0, D_FF + c0)]

    def conv(u, col):
        cols = slice(col, col + FF_CHUNK)
        before = pltpu.roll(u, 1, axis=0)[main]
        after = pltpu.roll(u, u.shape[0] - 1, axis=0)[main]
        return (before * cw_ref[0:1, cols] + u[main] * cw_ref[1:2, cols] + after * cw_ref[2:3, cols]
                + cb_ref[:, cols])

    starts = list(range(0, D_FF, FF_CHUNK))
    u_next = project(starts[0])
    for idx, c0 in enumerate(starts):
        u_a, u_b = u_next
        if idx + 1 < len(starts):
            u_next = project(starts[idx + 1])
        a, b = conv(u_a, c0), conv(u_b, D_FF + c0)
        act_ref[:, c0:c0 + FF_CHUNK] = (a / (1.0 + jnp.exp(-a)) * b).astype(BF16)
    o_ref[0] = x_ref[0] + jnp.dot(act_ref[...], wd_ref[...], preferred_element_type=F32)


def _ffn(x, g, w_up, conv_w, conv_b, w_down):
    B, S, D = x.shape
    tm = FFN_ROW_TILE
    halo_per_tile = tm // HALO_ROWS
    last_halo = S // HALO_ROWS - 1
    tile = pl.BlockSpec((1, tm, D), lambda b, i: (b, i, 0))
    return pl.pallas_call(
        _ffn_kernel,
        grid=(B, S // tm),
        in_specs=[tile,
                  pl.BlockSpec((1, HALO_ROWS, D), lambda b, i: (b, jnp.maximum(i * halo_per_tile - 1, 0), 0)),
                  pl.BlockSpec((1, HALO_ROWS, D),
                               lambda b, i: (b, jnp.minimum((i + 1) * halo_per_tile, last_halo), 0)),
                  _full(g.shape), _resident(w_up.shape), _full(conv_w.shape), _full(conv_b.shape),
                  _resident(w_down.shape)],
        out_specs=tile,
        out_shape=jax.ShapeDtypeStruct((B, S, D), F32),
        scratch_shapes=[pltpu.VMEM((tm + 2 * HALO_ROWS, D), BF16), pltpu.VMEM((tm, D_FF), BF16)],
        compiler_params=_params("parallel", "parallel"),
        name="ffn",
    )(x, x, x, g, w_up, conv_w, conv_b, w_down)


def _rope_tables(S):
    inv = ROPE_THETA ** (-(jnp.arange(ROT_HALF, dtype=F32) * 2.0 / (2 * ROT_HALF)))
    ang = jnp.arange(S, dtype=F32)[:, None] * inv[None, :]
    cos, sin = jnp.cos(ang), jnp.sin(ang)
    rest = HEAD_DIM - 2 * ROT_HALF
    one, zero = jnp.ones((S, rest), F32), jnp.zeros((S, rest), F32)
    zhalf = jnp.zeros((S, ROT_HALF), F32)
    per_head = (jnp.concatenate([cos, cos, one], axis=1),
                jnp.concatenate([zhalf, sin, zero], axis=1),
                jnp.concatenate([-sin, zhalf, zero], axis=1))
    return tuple(jnp.tile(t, (1, LANES // HEAD_DIM)) for t in per_head)


def _pair_gain(g, scale=1.0):
    g = jnp.atleast_2d(g.astype(F32) * scale)
    return jnp.tile(g, (1, LANES // HEAD_DIM))


def _group_ones(width):
    head = jnp.arange(width) // HEAD_DIM
    return (head[:, None] == head[None, :]).astype(BF16)


def _score_bound(q_gain, k_gain):
    return (NORM_SLACK * HEAD_DIM * Q_SCALE * jnp.max(jnp.abs(q_gain.astype(F32)))
            * jnp.max(jnp.abs(k_gain.astype(F32))))


def _a_weight(w):
    d = w.shape[0]
    n = A_HEADS * HEAD_DIM
    pair = lambda t: t.reshape(d, 2, A_HEADS, HEAD_DIM).transpose(0, 2, 1, 3).reshape(d, 2 * n)
    return jnp.concatenate([pair(w[:, :2 * n]), pair(w[:, 2 * n:4 * n]), w[:, 4 * n:]], axis=1).astype(BF16)


def _trunk(x, mem, P):
    S = x.shape[1]
    rope = _rope_tables(S)
    ones = _group_ones(MXU_COLS)
    n_layers = P["norm_mix"].shape[0]
    for i in range(n_layers):
        j = i // 2
        row = lambda name: P[name][i][None, :].astype(F32)
        mk, mv = _mem_kv(mem, row("norm_mem"), P["w_mem_kv"][i].astype(BF16), _group_ones(LANES),
                         _pair_gain(P["xk_norm"][i]))
        xg = _pair_gain(P["xq_norm"][i], Q_SCALE)
        if i % 2 == 0:
            qg = jnp.tile(_pair_gain(P["a_q_norm"][j], Q_SCALE), (A_HEADS, 1))
            kg = jnp.tile(_pair_gain(P["a_k_norm"][j]), (A_HEADS, 1))
            q, k, v, xq = _inproj(x, row("norm_mix"), _a_weight(P["a_w_in"][j]), ones, rope,
                                  qg, kg, xg, A_HEADS)
            lam_init = 0.8 - 0.6 * math.exp(-0.3 * i)
            subln = P["a_subln"][j][None, :].astype(F32) * (1.0 - lam_init)
            mixed = _diff_attn(q[:, 0], k[:, 0], v[:, 0], P["a_lambda"][j].astype(F32), subln, lam_init,
                               _score_bound(P["a_q_norm"][j], P["a_k_norm"][j]))
            w_out = P["a_w_out"][j]
        else:
            pairs = B_HEADS // 2
            width = B_HEADS * HEAD_DIM
            w_in = P["b_w_in"][j]
            outs, lses = [], []
            for gi, (window, dil) in enumerate(B_GROUPS):
                cols = [w_in[:, s * len(B_GROUPS) * width + gi * width:][:, :width] for s in range(3)]
                if gi == 0:
                    cols.append(w_in[:, 3 * len(B_GROUPS) * width:])
                qg = jnp.tile(_pair_gain(P["b_q_norm"][j][gi], Q_SCALE), (pairs, 1))
                kg = jnp.tile(_pair_gain(P["b_k_norm"][j][gi]), (pairs, 1))
                q, k, v, *rest = _inproj(x, row("norm_mix"), jnp.concatenate(cols, axis=1).astype(BF16),
                                         ones, rope, qg, kg, xg, pairs, dil=dil, with_xq=(gi == 0))
                if gi == 0:
                    xq = rest[0]
                o, l = _band_attn(q, k, v, (window // 2) // dil,
                                  _score_bound(P["b_q_norm"][j][gi], P["b_k_norm"][j][gi]))
                outs.append(o)
                lses.append(l)
            mixed = (outs, lses)
            w_out = P["b_w_out"][j]
        x = _outproj(mixed, xq, mk, mv, x, w_out.astype(BF16))
        x = _ffn(x, row("norm_ffn"), P["w_up"][i].astype(BF16), P["conv_w"][i].astype(F32),
                 P["conv_b"][i][None, :].astype(F32), P["w_down"][i].astype(BF16))
    return x


def kernel(x_prompt, x_sample, mem_prompt, mem_sample, norm_mix, norm_mem, w_mem_kv, xq_norm, xk_norm, a_w_in, a_w_out, a_q_norm, a_k_norm, a_lambda, a_subln, b_w_in, b_w_out, b_q_norm, b_k_norm, norm_ffn, w_up, conv_w, conv_b, w_down):
    P = dict(norm_mix=norm_mix, norm_mem=norm_mem, w_mem_kv=w_mem_kv, xq_norm=xq_norm, xk_norm=xk_norm,
             a_w_in=a_w_in, a_w_out=a_w_out, a_q_norm=a_q_norm, a_k_norm=a_k_norm, a_lambda=a_lambda,
             a_subln=a_subln, b_w_in=b_w_in, b_w_out=b_w_out, b_q_norm=b_q_norm, b_k_norm=b_k_norm,
             norm_ffn=norm_ffn, w_up=w_up, conv_w=conv_w, conv_b=conv_b, w_down=w_down)
    return _trunk(x_prompt, mem_prompt, P), _trunk(x_sample, mem_sample, P)
```
